```python
import math
import jax
import jax.numpy as jnp
from jax import lax
import numpy as np

D_MODEL = 1024
BATCH = 4
SEQ = 8192
DEPTH = 4

N_AB = (DEPTH + 1) // 2
N_CD = DEPTH // 2

SSD_HEADS = 8
SSD_HEAD_DIM = 64
SSD_D_INNER = SSD_HEADS * SSD_HEAD_DIM
SSD_GROUPS = 2
SSD_D_STATE = 64
SSD_CONV = 4
SSD_CHUNK = 128
SSD_CONV_DIM = SSD_D_INNER + 2 * SSD_GROUPS * SSD_D_STATE

HG_HEADS = 4
HG_KEY_DIM = 128
HG_VAL_DIM = 128
HG_WIDTH = HG_HEADS * HG_KEY_DIM
HG_CHUNK = 64

SWA_Q_HEADS = 8
SWA_KV_HEADS = 2
SWA_HEAD_DIM = 64
SWA_WINDOW = 128
SWA_BLOCK = 128

RG_WIDTH = 512
RG_BLOCKS = 8
RG_BLOCK_DIM = RG_WIDTH // RG_BLOCKS
RG_CONV = 4
RG_C = 8.0

FFN_DIM = 2816
FFN_CONV = 3

LN_EPS = 1e-5
RMS_EPS = 1e-6
MASK_VALUE = -1e9
ALPHA = (2 * DEPTH) ** 0.25
BETA = (8 * DEPTH) ** -0.25

AB_SIZES = (SSD_D_INNER, SSD_CONV_DIM, SSD_HEADS, HG_WIDTH, HG_WIDTH, HG_HEADS * HG_VAL_DIM, HG_HEADS * HG_VAL_DIM)
AB_IN = sum(AB_SIZES)
AB_OUT_IN = SSD_D_INNER + HG_HEADS * HG_VAL_DIM
CD_SIZES = (SWA_Q_HEADS * SWA_HEAD_DIM, SWA_KV_HEADS * SWA_HEAD_DIM, SWA_KV_HEADS * SWA_HEAD_DIM, RG_WIDTH, RG_WIDTH)
CD_IN = sum(CD_SIZES)
CD_OUT_IN = SWA_Q_HEADS * SWA_HEAD_DIM + RG_WIDTH

kernel_name = 'hybrid_ssd_hgrn2_swa_rglru_deepnorm'


def _layer_norm(x, g, b):
    xf = x.astype(jnp.float32)
    mu = jnp.mean(xf, -1, keepdims=True)
    var = jnp.mean(jnp.square(xf - mu), -1, keepdims=True)
    return ((xf - mu) * lax.rsqrt(var + LN_EPS) * g.astype(jnp.float32) + b.astype(jnp.float32)).astype(x.dtype)


def _rms_norm(x, w):
    xf = x.astype(jnp.float32)
    return xf * lax.rsqrt(jnp.mean(jnp.square(xf), -1, keepdims=True) + RMS_EPS) * w.astype(jnp.float32)


def _split(h, sizes):
    return jnp.split(h, np.cumsum(sizes)[:-1].tolist(), axis=-1)


def _causal_dwconv(x, w, b):
    width = w.shape[0]
    y = lax.conv_general_dilated(x, w[:, None, :].astype(x.dtype), window_strides=(1,),
                                 padding=((width - 1, 0),), dimension_numbers=('NWC', 'WIO', 'NWC'),
                                 feature_group_count=x.shape[-1])
    return y + b.astype(x.dtype)


def _masked_exp(diff, mask):
    return jnp.where(mask, jnp.exp(jnp.where(mask, diff, 0.0)), 0.0)


def _decay_matrix(cs):
    t = cs.shape[-1]
    mask = jnp.tril(jnp.ones((t, t), dtype=bool))
    return _masked_exp(cs[..., :, None] - cs[..., None, :], mask)


def _ssd_scan(x, dt, a, bm, cm):
    b, s, h, p = x.shape
    g, n = bm.shape[-2:]
    L = SSD_CHUNK
    nc = s // L
    rep = h // g
    bh = jnp.repeat(bm, rep, axis=2).reshape(b, nc, L, h, n)
    ch = jnp.repeat(cm, rep, axis=2).reshape(b, nc, L, h, n)
    xc = (x * dt[..., None]).reshape(b, nc, L, h, p)
    a_cs = jnp.cumsum((dt * a).reshape(b, nc, L, h).transpose(0, 3, 1, 2), axis=-1)
    scores = jnp.einsum('bclhn,bcshn->bhcls', ch, bh) * _decay_matrix(a_cs)
    y_diag = jnp.einsum('bhcls,bcshp->bclhp', scores, xc)
    decay_states = jnp.exp(a_cs[..., -1:] - a_cs)
    states = jnp.einsum('bclhn,bhcl,bclhp->bchpn', bh, decay_states, xc)
    states = jnp.concatenate([jnp.zeros_like(states[:, :1]), states], axis=1)
    chunk_cs = jnp.cumsum(jnp.pad(a_cs[..., -1], ((0, 0), (0, 0), (1, 0))), axis=-1)
    states = jnp.einsum('bhzc,bchpn->bzhpn', _decay_matrix(chunk_cs), states)[:, :-1]
    y_off = jnp.einsum('bclhn,bchpn,bhcl->bclhp', ch, states, jnp.exp(a_cs))
    return (y_diag + y_off).reshape(b, s, h, p)


def _ssd_mixer(z, xbc, dt_raw, conv_w, conv_b, dt_bias, a_log, d_skip, norm_w):
    bsz, seq, _ = z.shape
    xbc = jax.nn.silu(_causal_dwconv(xbc, conv_w, conv_b)).astype(jnp.float32)
    xs, bm, cm = _split(xbc, (SSD_D_INNER, SSD_GROUPS * SSD_D_STATE, SSD_GROUPS * SSD_D_STATE))
    xs = xs.reshape(bsz, seq, SSD_HEADS, SSD_HEAD_DIM)
    bm = bm.reshape(bsz, seq, SSD_GROUPS, SSD_D_STATE)
    cm = cm.reshape(bsz, seq, SSD_GROUPS, SSD_D_STATE)
    dt = jax.nn.softplus(dt_raw.astype(jnp.float32) + dt_bias.astype(jnp.float32))
    a = -jnp.exp(a_log.astype(jnp.float32))
    y = _ssd_scan(xs, dt, a, bm, cm) + d_skip.astype(jnp.float32)[:, None] * xs
    y = y.reshape(bsz, seq, SSD_D_INNER) * jax.nn.silu(z.astype(jnp.float32))
    y = _rms_norm(y.reshape(bsz, seq, SSD_GROUPS, -1), norm_w.reshape(SSD_GROUPS, -1))
    return y.reshape(bsz, seq, SSD_D_INNER)


def _hgrn2_scan(q, k, v, log_f):
    b, s, h, dk = q.shape
    dv = v.shape[-1]
    L = HG_CHUNK
    nc = s // L
    to_chunks = lambda t: t.reshape(b, nc, L, h, t.shape[-1]).transpose(1, 0, 3, 2, 4)
    mask = jnp.tril(jnp.ones((L, L), dtype=bool))[:, :, None]

    def step(state, inp):
        qc, kc, vc, gc = inp
        bc = jnp.cumsum(gc, axis=2)
        decay = _masked_exp(bc[:, :, :, None, :] - bc[:, :, None, :, :], mask)
        attn = jnp.einsum('bhtk,bhsk,bhtsk->bhts', qc, kc, decay)
        out = jnp.einsum('bhts,bhsv->bhtv', attn, vc) + jnp.einsum('bhtk,bhkv->bhtv', qc * jnp.exp(bc), state)
        b_last = bc[:, :, -1:, :]
        state = jnp.exp(b_last[:, :, 0])[..., None] * state + jnp.einsum('bhsk,bhsv->bhkv', kc * jnp.exp(b_last - bc), vc)
        return state, out

    state0 = jnp.zeros((b, h, dk, dv), jnp.float32)
    _, out = lax.scan(step, state0, (to_chunks(q), to_chunks(k), to_chunks(v), to_chunks(log_f)))
    return out.transpose(1, 0, 3, 2, 4).reshape(b, s, h, dv)


def _hgrn2_mixer(hq, hf, hi, hg, lb, norm_w):
    bsz, seq, _ = hq.shape
    q = jax.nn.silu(hq.astype(jnp.float32)).reshape(bsz, seq, HG_HEADS, HG_KEY_DIM)
    fx = hf.astype(jnp.float32).reshape(bsz, seq, HG_HEADS, HG_KEY_DIM)
    lb = lb.reshape(HG_HEADS, HG_KEY_DIM)
    log_f = jnp.log(lb + (1.0 - lb) * jax.nn.sigmoid(fx))
    k = (1.0 - lb) * jax.nn.sigmoid(-fx)
    v = hi.astype(jnp.float32).reshape(bsz, seq, HG_HEADS, HG_VAL_DIM)
    o = _hgrn2_scan(q, k, v, log_f)
    o = _rms_norm(o, norm_w) * jax.nn.silu(hg.astype(jnp.float32).reshape(bsz, seq, HG_HEADS, HG_VAL_DIM))
    return o.reshape(bsz, seq, HG_HEADS * HG_VAL_DIM)


def _swa_sink_attention(q, k, v, sinks):
    b, s, hq, d = q.shape
    hkv = k.shape[2]
    grp = hq // hkv
    T = SWA_BLOCK
    nb = s // T
    qb = q.reshape(b, nb, T, hkv, grp, d)

    def banded(t):
        tb = t.reshape(b, nb, T, hkv, d)
        prev = jnp.pad(tb, ((0, 0), (1, 0), (0, 0), (0, 0), (0, 0)))[:, :-1]
        return jnp.concatenate([prev, tb], axis=2)

    kb, vb = banded(k), banded(v)
    scores = jnp.einsum('bnqhgd,bnkhd->bnhgqk', qb, kb).astype(jnp.float32) * (d ** -0.5)
    rel = (jnp.arange(T)[:, None] + T) - jnp.arange(2 * T)[None, :]
    band = (rel >= 0) & (rel < SWA_WINDOW)
    valid = (jnp.arange(nb)[:, None] > 0) | (jnp.arange(2 * T)[None, :] >= T)
    mask = band[None] & valid[:, None, :]
    scores = jnp.where(mask[None, :, None, None], scores, MASK_VALUE)
    sink = sinks.astype(jnp.float32).reshape(hkv, grp)
    sink_col = jnp.broadcast_to(sink[None, None, :, :, None, None], scores.shape[:-1] + (1,))
    probs = jax.nn.softmax(jnp.concatenate([scores, sink_col], axis=-1), axis=-1)[..., :-1]
    out = jnp.einsum('bnhgqk,bnkhd->bnqhgd', probs.astype(v.dtype), vb)
    return out.reshape(b, s, hq * d)


def _rglru_mixer(gate, xr, conv_w, conv_b, wa, ba, wx, bx, lam):
    bsz, seq, _ = xr.shape
    xc = _causal_dwconv(xr, conv_w, conv_b).astype(jnp.float32)
    xblk = xc.reshape(bsz, seq, RG_BLOCKS, RG_BLOCK_DIM)
    r = jax.nn.sigmoid(jnp.einsum('bsgi,gij->bsgj', xblk, wa.astype(jnp.float32)).reshape(bsz, seq, RG_WIDTH) + ba.astype(jnp.float32))
    i = jax.nn.sigmoid(jnp.einsum('bsgi,gij->bsgj', xblk, wx.astype(jnp.float32)).reshape(bsz, seq, RG_WIDTH) + bx.astype(jnp.float32))
    log_a = -RG_C * r * jax.nn.softplus(-lam.astype(jnp.float32))
    a = jnp.exp(log_a)
    u = jnp.sqrt(jnp.maximum(-jnp.expm1(2.0 * log_a), 0.0)) * (i * xc)

    def combine(c1, c2):
        a1, u1 = c1
        a2, u2 = c2
        return a1 * a2, a2 * u1 + u2

    _, h = lax.associative_scan(combine, (a, u), axis=1)
    return h * jax.nn.gelu(gate.astype(jnp.float32))


def _ab_mixer(x, w_in, conv_w, conv_b, dt_bias, a_log, d_skip, ssd_norm_w, lb, hg_norm_w, w_out):
    h = x @ w_in
    z, xbc, dt_raw, hq, hf, hi, hg = _split(h, AB_SIZES)
    y_a = _ssd_mixer(z, xbc, dt_raw, conv_w, conv_b, dt_bias, a_log, d_skip, ssd_norm_w)
    y_b = _hgrn2_mixer(hq, hf, hi, hg, lb, hg_norm_w)
    return jnp.concatenate([y_a, y_b], axis=-1).astype(x.dtype) @ w_out


def _cd_mixer(x, w_in, sinks, conv_w, conv_b, wa, ba, wx, bx, lam, w_out):
    bsz, seq, _ = x.shape
    h = x @ w_in
    q, k, v, gate, xr = _split(h, CD_SIZES)
    y_c = _swa_sink_attention(q.reshape(bsz, seq, SWA_Q_HEADS, SWA_HEAD_DIM),
                              k.reshape(bsz, seq, SWA_KV_HEADS, SWA_HEAD_DIM),
                              v.reshape(bsz, seq, SWA_KV_HEADS, SWA_HEAD_DIM), sinks)
    y_d = _rglru_mixer(gate, xr, conv_w, conv_b, wa, ba, wx, bx, lam)
    return jnp.concatenate([y_c.astype(jnp.float32), y_d], axis=-1).astype(x.dtype) @ w_out


def _conv_ffn(x, w_up, conv_w, conv_b, w_down):
    h = _causal_dwconv(x @ w_up, conv_w, conv_b)
    g, u = jnp.split(h, 2, axis=-1)
    return (jax.nn.silu(g) * u) @ w_down


def setup_inputs(seed: int = 0) -> dict:
    key = jax.random.key(seed)
    keys = list(jax.random.split(key, 40))

    def nrm(shape, scale):
        return jax.random.normal(keys.pop(), shape, jnp.float32) * scale

    def unif(shape, lo, hi):
        return jax.random.uniform(keys.pop(), shape, jnp.float32, minval=lo, maxval=hi)

    x = nrm((BATCH, SEQ, D_MODEL), 1.0)
    ab_w_in = nrm((N_AB, D_MODEL, AB_IN), D_MODEL ** -0.5)
    ssd_conv_w = nrm((N_AB, SSD_CONV, SSD_CONV_DIM), SSD_CONV ** -0.5)
    ssd_conv_b = nrm((N_AB, SSD_CONV_DIM), 0.02)
    dt0 = jnp.exp(unif((N_AB, SSD_HEADS), math.log(1e-3), math.log(1e-1)))
    ssd_dt_bias = dt0 + jnp.log(-jnp.expm1(-dt0))
    ssd_a_log = jnp.log(unif((N_AB, SSD_HEADS), 1.0, 16.0))
    ssd_d = 1.0 + nrm((N_AB, SSD_HEADS), 0.1)
    ssd_norm_w = 1.0 + nrm((N_AB, SSD_D_INNER), 0.1)
    hg_lower = nrm((N_AB, HG_WIDTH), 1.0)
    hg_norm_w = 1.0 + nrm((N_AB, HG_VAL_DIM), 0.1)
    ab_w_out = nrm((N_AB, AB_OUT_IN, D_MODEL), (AB_OUT_IN ** -0.5) * BETA)
    cd_w_in = nrm((N_CD, D_MODEL, CD_IN), D_MODEL ** -0.5)
    swa_sinks = nrm((N_CD, SWA_Q_HEADS), 0.5)
    rg_conv_w = nrm((N_CD, RG_CONV, RG_WIDTH), RG_CONV ** -0.5)
    rg_conv_b = nrm((N_CD, RG_WIDTH), 0.02)
    rg_wa = nrm((N_CD, RG_BLOCKS, RG_BLOCK_DIM, RG_BLOCK_DIM), RG_BLOCK_DIM ** -0.5)
    rg_ba = nrm((N_CD, RG_WIDTH), 0.02)
    rg_wx = nrm((N_CD, RG_BLOCKS, RG_BLOCK_DIM, RG_BLOCK_DIM), RG_BLOCK_DIM ** -0.5)
    rg_bx = nrm((N_CD, RG_WIDTH), 0.02)
    sig = unif((N_CD, RG_WIDTH), 0.9, 0.999) ** (1.0 / RG_C)
    rg_lambda = jnp.log(sig) - jnp.log1p(-sig)
    cd_w_out = nrm((N_CD, CD_OUT_IN, D_MODEL), (CD_OUT_IN ** -0.5) * BETA)
    ffn_w_up = nrm((DEPTH, D_MODEL, 2 * FFN_DIM), D_MODEL ** -0.5)
    ffn_conv_w = nrm((DEPTH, FFN_CONV, 2 * FFN_DIM), FFN_CONV ** -0.5)
    ffn_conv_b = nrm((DEPTH, 2 * FFN_DIM), 0.02)
    ffn_w_down = nrm((DEPTH, FFN_DIM, D_MODEL), (FFN_DIM ** -0.5) * BETA)
    ln_g = 1.0 + nrm((DEPTH, 2, D_MODEL), 0.05)
    ln_b = nrm((DEPTH, 2, D_MODEL), 0.02)
    return {'x': x, 'ab_w_in': ab_w_in, 'ssd_conv_w': ssd_conv_w, 'ssd_conv_b': ssd_conv_b,
            'ssd_dt_bias': ssd_dt_bias, 'ssd_a_log': ssd_a_log, 'ssd_d': ssd_d, 'ssd_norm_w': ssd_norm_w,
            'hg_lower': hg_lower, 'hg_norm_w': hg_norm_w, 'ab_w_out': ab_w_out, 'cd_w_in': cd_w_in,
            'swa_sinks': swa_sinks, 'rg_conv_w': rg_conv_w, 'rg_conv_b': rg_conv_b, 'rg_wa': rg_wa,
            'rg_ba': rg_ba, 'rg_wx': rg_wx, 'rg_bx': rg_bx, 'rg_lambda': rg_lambda, 'cd_w_out': cd_w_out,
            'ffn_w_up': ffn_w_up, 'ffn_conv_w': ffn_conv_w, 'ffn_conv_b': ffn_conv_b, 'ffn_w_down': ffn_w_down,
            'ln_g': ln_g, 'ln_b': ln_b}


def reference(x, ab_w_in, ssd_conv_w, ssd_conv_b, ssd_dt_bias, ssd_a_log, ssd_d, ssd_norm_w,
              hg_lower, hg_norm_w, ab_w_out, cd_w_in, swa_sinks, rg_conv_w, rg_conv_b, rg_wa,
              rg_ba, rg_wx, rg_bx, rg_lambda, cd_w_out, ffn_w_up, ffn_conv_w, ffn_conv_b, ffn_w_down,
              ln_g, ln_b):
    sm = jax.nn.softmax(hg_lower.astype(jnp.float32), axis=0)
    lb_all = jnp.clip(jnp.cumsum(sm, axis=0) - sm[0], 0.0, 1.0)
    for layer in range(DEPTH):
        j = layer // 2
        if layer % 2 == 0:
            m = _ab_mixer(x, ab_w_in[j], ssd_conv_w[j], ssd_conv_b[j], ssd_dt_bias[j], ssd_a_log[j],
                          ssd_d[j], ssd_norm_w[j], lb_all[j], hg_norm_w[j], ab_w_out[j])
        else:
            m = _cd_mixer(x, cd_w_in[j], swa_sinks[j], rg_conv_w[j], rg_conv_b[j], rg_wa[j], rg_ba[j],
                          rg_wx[j], rg_bx[j], rg_lambda[j], cd_w_out[j])
        x = _layer_norm(ALPHA * x + m, ln_g[layer, 0], ln_b[layer, 0])
        f = _conv_ffn(x, ffn_w_up[layer], ffn_conv_w[layer], ffn_conv_b[layer], ffn_w_down[layer])
        x = _layer_norm(ALPHA * x + f, ln_g[layer, 1], ln_b[layer, 1])
    return x
```

```python
import functools

import jax
import jax.numpy as jnp
import numpy as np
from jax import lax
from jax.experimental import pallas as pl
from jax.experimental.pallas import tpu as pltpu

F32 = jnp.float32
BF16 = jnp.bfloat16
HIGHEST = lax.Precision.HIGHEST

LANES = 128
SUBLANES = 8
VMEM_LIMIT_BYTES = 56 * 1024 * 1024

D_MODEL = 1024
DEPTH = 4
SSD_HEADS = 8
SSD_HEAD_DIM = 64
SSD_D_INNER = SSD_HEADS * SSD_HEAD_DIM
SSD_GROUPS = 2
SSD_D_STATE = 64
SSD_CONV = 4
SSD_CONV_DIM = SSD_D_INNER + 2 * SSD_GROUPS * SSD_D_STATE
SSD_CHUNK = 128
HG_HEADS = 4
HG_KEY_DIM = 128
HG_VAL_DIM = 128
HG_WIDTH = HG_HEADS * HG_KEY_DIM
HG_CHUNK = 64
HG_SAFE_DECAY = 160.0
SWA_Q_HEADS = 8
SWA_KV_HEADS = 2
SWA_HEAD_DIM = 64
SWA_BLOCK = 128
SWA_Q_DIM = SWA_Q_HEADS * SWA_HEAD_DIM
SWA_KV_DIM = SWA_KV_HEADS * SWA_HEAD_DIM
RG_WIDTH = 512
RG_BLOCKS = 8
RG_BLOCK_DIM = RG_WIDTH // RG_BLOCKS
RG_CONV = 4
RG_C = 8.0
RG_CHUNK = 128
FFN_DIM = 2816
FFN_CONV = 3
FFN_COL_CHUNK = 256
ROW_TILE = 512
LN_EPS = 1e-5
RMS_EPS = 1e-6
MASK_VALUE = -1e9
ALPHA = (2 * DEPTH) ** 0.25


def _params(n_axes):
    return pltpu.CompilerParams(dimension_semantics=("arbitrary",) * n_axes,
                                vmem_limit_bytes=VMEM_LIMIT_BYTES)


def _const_spec(shape):
    return pl.BlockSpec(shape, lambda *_: (0,) * len(shape), pipeline_mode=pl.Buffered(1))


def _sigmoid(x):
    return 1.0 / (1.0 + jnp.exp(-x))


def _silu(x):
    return x * _sigmoid(x)


def _softplus(x):
    return jnp.maximum(x, 0.0) + jnp.log1p(jnp.exp(-jnp.abs(x)))


def _gelu_tanh(x):
    return 0.5 * x * (1.0 + jnp.tanh(np.sqrt(2.0 / np.pi).astype(np.float32) * (x + 0.044715 * (x * x * x))))


def _layer_norm(r, g, b):
    mu = jnp.mean(r, axis=-1, keepdims=True)
    d = r - mu
    var = jnp.mean(d * d, axis=-1, keepdims=True)
    return d * lax.rsqrt(var + LN_EPS) * g + b


def _dot(a, b):
    return jnp.dot(a, b, preferred_element_type=F32)


def _dot_exact(a, b):
    return jnp.dot(a, b, preferred_element_type=F32, precision=HIGHEST)


def _dot_nt(a, b):
    return lax.dot_general(a, b, (((1,), (1,)), ((), ())), preferred_element_type=F32)


def _dot_tn(a, b):
    return lax.dot_general(a, b, (((0,), (0,)), ((), ())), preferred_element_type=F32)


def _lower_tri(n):
    row = lax.broadcasted_iota(jnp.int32, (n, n), 0)
    col = lax.broadcasted_iota(jnp.int32, (n, n), 1)
    return col <= row


def _inproj_kernel(x_ref, w_ref, *o_refs, splits):
    xb = x_ref[...].astype(BF16)
    off = 0
    for o_ref, n in zip(o_refs, splits):
        o_ref[...] = _dot(xb, w_ref[:, off:off + n]).astype(o_ref.dtype)
        off += n


def _inproj(x, w, splits, dtypes):
    rows, d = x.shape
    n_total = sum(splits)
    return pl.pallas_call(
        functools.partial(_inproj_kernel, splits=splits),
        grid=(rows // ROW_TILE,),
        in_specs=[pl.BlockSpec((ROW_TILE, d), lambda i: (i, 0)), _const_spec((d, n_total))],
        out_specs=[pl.BlockSpec((ROW_TILE, n), lambda i: (i, 0)) for n in splits],
        out_shape=[jax.ShapeDtypeStruct((rows, n), dt) for n, dt in zip(splits, dtypes)],
        compiler_params=_params(1),
        name="inproj",
    )(x, w)


def _ssd_kernel(z_ref, xbc_ref, dt_ref, cw_ref, cb_ref, dtb_ref, alog_ref, dskip_ref, nw_ref, expand_ref,
                o_ref, state_ref, xcat_ref):
    t = SSD_CHUNK
    halo = SUBLANES

    @pl.when(pl.program_id(1) == 0)
    def _():
        state_ref[...] = jnp.zeros_like(state_ref)
        xcat_ref[0:halo, :] = jnp.zeros((halo, SSD_CONV_DIM), F32)

    xcat_ref[halo:halo + t, :] = xbc_ref[...]
    cw = cw_ref[...]
    conv = cb_ref[...] + cw[3:4, :] * xcat_ref[halo:halo + t, :]
    for k in range(1, SSD_CONV):
        conv = conv + cw[3 - k:4 - k, :] * xcat_ref[halo - k:halo - k + t, :]
    xcat_ref[0:halo, :] = xcat_ref[t:t + halo, :]
    xa = _silu(conv)
    xs = xa[:, :SSD_D_INNER]
    bm = xa[:, SSD_D_INNER:SSD_D_INNER + LANES]
    cm = xa[:, SSD_D_INNER + LANES:SSD_D_INNER + 2 * LANES]

    dt = _softplus(dt_ref[...] + dtb_ref[...])
    da = dt * (-jnp.exp(alog_ref[...]))
    tri = _lower_tri(t)
    cs = _dot_exact(tri.astype(F32), da)
    cs_row = cs.T
    expand = expand_ref[...]
    dt_f = _dot_exact(dt, expand)
    cs_f = _dot_exact(cs, expand)
    cs_last = cs_f[t - 1:t, :]
    xdt = xs * dt_f
    ecs = jnp.exp(cs_f)
    w_state = (jnp.exp(cs_last - cs_f) * xdt).astype(BF16)
    dec_last = jnp.exp(cs_last)

    lane = lax.broadcasted_iota(jnp.int32, (t, LANES), 1)
    lo_half = lane < SSD_D_STATE
    sub = lax.broadcasted_iota(jnp.int32, (LANES, LANES), 0)
    bm_t = bm.T.astype(BF16)
    cm_g = [jnp.where(lo_half, cm, 0.0).astype(BF16), jnp.where(lo_half, 0.0, cm).astype(BF16)]
    scores_g = [_dot(c, bm_t) for c in cm_g]

    y_pairs = []
    heads_per_group = SSD_HEADS // SSD_GROUPS
    for pair in range(SSD_HEADS // 2):
        g = (2 * pair) // heads_per_group
        sl = slice(LANES * pair, LANES * (pair + 1))
        x_pair = xdt[:, sl]
        y_pair = None
        for j in range(2):
            h = 2 * pair + j
            x_h = jnp.where(lo_half if j == 0 else jnp.logical_not(lo_half), x_pair, 0.0).astype(BF16)
            diff = cs[:, h:h + 1] - cs_row[h:h + 1, :]
            decay = jnp.where(tri, jnp.exp(jnp.where(tri, diff, 0.0)), 0.0)
            y_h = _dot((scores_g[g] * decay).astype(BF16), x_h)
            y_pair = y_h if y_pair is None else y_pair + y_h
        state = state_ref[pair]
        y_pair = y_pair + _dot(cm_g[g], state.astype(BF16)) * ecs[:, sl]
        upd = _dot(bm_t, w_state[:, sl])
        in_group = (sub >= g * SSD_D_STATE) & (sub < (g + 1) * SSD_D_STATE)
        state_ref[pair] = jnp.where(in_group, upd, 0.0) + dec_last[:, sl] * state
        y_pairs.append(y_pair)

    y = jnp.concatenate(y_pairs, axis=-1) + dskip_ref[...] * xs
    y = y * _silu(z_ref[...])
    gw = SSD_D_INNER // SSD_GROUPS
    outs = []
    for g in range(SSD_GROUPS):
        yg = y[:, g * gw:(g + 1) * gw]
        ms = jnp.mean(yg * yg, axis=-1, keepdims=True)
        outs.append(yg * lax.rsqrt(ms + RMS_EPS) * nw_ref[:, g * gw:(g + 1) * gw])
    o_ref[...] = jnp.concatenate(outs, axis=-1).astype(o_ref.dtype)


def _ssd_mixer(z, xbc, dt_raw, conv_w, conv_b, dt_bias, a_log, d_skip, norm_w, batch):
    rows = z.shape[0]
    nchunk = rows // batch // SSD_CHUNK
    pad_heads = lambda v: jnp.pad(v.astype(F32), (0, LANES - SSD_HEADS)).reshape(1, LANES)
    expand = (jnp.arange(LANES)[:, None] == (jnp.arange(SSD_D_INNER) // SSD_HEAD_DIM)[None, :]).astype(F32)
    row_spec = lambda n: pl.BlockSpec((SSD_CHUNK, n), lambda b, c: (b * nchunk + c, 0))
    return pl.pallas_call(
        _ssd_kernel,
        grid=(batch, nchunk),
        in_specs=[row_spec(SSD_D_INNER), row_spec(SSD_CONV_DIM), row_spec(LANES),
                  _const_spec((SSD_CONV, SSD_CONV_DIM)), _const_spec((1, SSD_CONV_DIM)),
                  _const_spec((1, LANES)), _const_spec((1, LANES)),
                  _const_spec((1, SSD_D_INNER)), _const_spec((1, SSD_D_INNER)),
                  _const_spec((LANES, SSD_D_INNER))],
        out_specs=row_spec(SSD_D_INNER),
        out_shape=jax.ShapeDtypeStruct((rows, SSD_D_INNER), BF16),
        scratch_shapes=[pltpu.VMEM((SSD_HEADS // 2, LANES, LANES), F32),
                        pltpu.VMEM((SSD_CHUNK + SUBLANES, SSD_CONV_DIM), F32)],
        compiler_params=_params(2),
        name="ssd_mixer",
    )(z, xbc, dt_raw, conv_w.astype(F32), conv_b.astype(F32).reshape(1, -1), pad_heads(dt_bias), pad_heads(a_log),
      jnp.repeat(d_skip.astype(F32), SSD_HEAD_DIM).reshape(1, -1), norm_w.astype(F32).reshape(1, -1), expand)


def _hgrn2_kernel(hq_ref, hf_ref, hi_ref, hg_ref, lower_ref, nw_ref, o_ref,
                  state_ref, attn_ref, *, layer):
    t = HG_CHUNK

    @pl.when(pl.program_id(1) == 0)
    def _():
        state_ref[...] = jnp.zeros_like(state_ref)

    low = lower_ref[...]
    ex = jnp.exp(low - jnp.max(low, axis=0, keepdims=True))
    sm = ex / jnp.sum(ex, axis=0, keepdims=True)
    lb = jnp.clip(jnp.sum(sm[:layer + 1, :], axis=0, keepdims=True) - sm[0:1, :], 0.0, 1.0)

    q = _silu(hq_ref[...])
    fx = hf_ref[...]
    log_f = jnp.log(lb + (1.0 - lb) * _sigmoid(fx))
    k = (1.0 - lb) * _sigmoid(-fx)
    vb = hi_ref[...].astype(BF16)

    tri = _lower_tri(t)
    bc = _dot_exact(tri.astype(F32), log_f)
    b_last = bc[t - 1:t, :]
    q_state = (q * jnp.exp(bc)).astype(BF16)
    k_state = (k * jnp.exp(b_last - bc)).astype(BF16)
    dec_last = jnp.exp(b_last)

    safe = jnp.max(-b_last) <= HG_SAFE_DECAY

    @pl.when(safe)
    def _():
        mid = 0.5 * b_last
        q_mid = (q * jnp.exp(bc - mid)).astype(BF16)
        k_mid = (k * jnp.exp(mid - bc)).astype(BF16)
        for h in range(HG_HEADS):
            sl = slice(h * HG_KEY_DIM, (h + 1) * HG_KEY_DIM)
            attn_ref[h] = _dot_nt(q_mid[:, sl], k_mid[:, sl])

    @pl.when(jnp.logical_not(safe))
    def _():
        col_id = lax.broadcasted_iota(jnp.int32, (t, t), 1)
        row_id = lax.broadcasted_iota(jnp.int32, (t, HG_KEY_DIM), 0)
        for h in range(HG_HEADS):
            sl = slice(h * HG_KEY_DIM, (h + 1) * HG_KEY_DIM)
            q_h = q[:, sl]
            k_h = k[:, sl]
            bc_h = bc[:, sl]

            def column(s, acc, q_h=q_h, k_h=k_h, bc_h=bc_h):
                k_s = jnp.sum(jnp.where(row_id == s, k_h, 0.0), axis=0, keepdims=True)
                bc_s = jnp.sum(jnp.where(row_id == s, bc_h, 0.0), axis=0, keepdims=True)
                w = q_h * k_s * jnp.exp(jnp.minimum(bc_h - bc_s, 0.0))
                return jnp.where(col_id == s, jnp.sum(w, axis=-1, keepdims=True), acc)

            attn_ref[h] = lax.fori_loop(0, t, column, jnp.zeros((t, t), F32))

    outs = []
    for h in range(HG_HEADS):
        sl = slice(h * HG_KEY_DIM, (h + 1) * HG_KEY_DIM)
        attn = jnp.where(tri, attn_ref[h], 0.0).astype(BF16)
        state = state_ref[h]
        o = _dot(attn, vb[:, sl]) + _dot_nt(q_state[:, sl], state.astype(BF16))
        state_ref[h] = state * dec_last[:, sl] + _dot_tn(vb[:, sl], k_state[:, sl])
        ms = jnp.mean(o * o, axis=-1, keepdims=True)
        outs.append(o * lax.rsqrt(ms + RMS_EPS) * nw_ref[...])
    o_ref[...] = (jnp.concatenate(outs, axis=-1) * _silu(hg_ref[...])).astype(o_ref.dtype)


def _hgrn2_mixer(hq, hf, hi, hg, hg_lower, norm_w, layer, batch):
    rows = hq.shape[0]
    nchunk = rows // batch // HG_CHUNK
    row_spec = pl.BlockSpec((HG_CHUNK, HG_WIDTH), lambda b, c: (b * nchunk + c, 0))
    return pl.pallas_call(
        functools.partial(_hgrn2_kernel, layer=layer),
        grid=(batch, nchunk),
        in_specs=[row_spec, row_spec, row_spec, row_spec,
                  _const_spec(hg_lower.shape), _const_spec((1, HG_VAL_DIM))],
        out_specs=row_spec,
        out_shape=jax.ShapeDtypeStruct((rows, HG_WIDTH), BF16),
        scratch_shapes=[pltpu.VMEM((HG_HEADS, HG_VAL_DIM, HG_KEY_DIM), F32),
                        pltpu.VMEM((HG_HEADS, HG_CHUNK, HG_CHUNK), F32)],
        compiler_params=_params(2),
        name="hgrn2_mixer",
    )(hq, hf, hi, hg, hg_lower.astype(F32), norm_w.astype(F32).reshape(1, -1))


def _swa_kernel(sinks_ref, q_ref, kp_ref, kc_ref, vp_ref, vc_ref, o_ref):
    t = SWA_BLOCK
    q = q_ref[...] * (SWA_HEAD_DIM ** -0.5)
    kcat = jnp.concatenate([kp_ref[...], kc_ref[...]], axis=0)
    vcat = jnp.concatenate([vp_ref[...], vc_ref[...]], axis=0)
    lane_kv = lax.broadcasted_iota(jnp.int32, (2 * t, LANES), 1)
    lo_kv = lane_kv < SWA_HEAD_DIM
    k_swap = pltpu.roll(kcat, SWA_HEAD_DIM, axis=1)
    v_swap = pltpu.roll(vcat, SWA_HEAD_DIM, axis=1)
    k_dup = [jnp.where(lo_kv, kcat, k_swap).astype(BF16), jnp.where(lo_kv, k_swap, kcat).astype(BF16)]
    v_dup = [jnp.where(lo_kv, vcat, v_swap), jnp.where(lo_kv, v_swap, vcat)]

    row = lax.broadcasted_iota(jnp.int32, (t, 2 * t), 0)
    col = lax.broadcasted_iota(jnp.int32, (t, 2 * t), 1)
    first_block = pl.program_id(1) == 0
    visible = (col > row) & (col <= row + t) & ((col >= t) | jnp.logical_not(first_block))
    lane_q = lax.broadcasted_iota(jnp.int32, (t, LANES), 1)
    lo_q = lane_q < SWA_HEAD_DIM

    group = SWA_Q_HEADS // SWA_KV_HEADS
    outs = []
    for pair in range(SWA_Q_HEADS // 2):
        g = (2 * pair) // group
        q_pair = q[:, pair * LANES:(pair + 1) * LANES]
        acc = None
        for j in range(2):
            h = 2 * pair + j
            half_q = lo_q if j == 0 else jnp.logical_not(lo_q)
            half_kv = lo_kv if j == 0 else jnp.logical_not(lo_kv)
            s = _dot_nt(jnp.where(half_q, q_pair, 0.0).astype(BF16), k_dup[g])
            s = jnp.where(visible, s, MASK_VALUE)
            sink = sinks_ref[h]
            m = jnp.maximum(jnp.max(s, axis=-1, keepdims=True), sink)
            p = jnp.exp(s - m)
            denom = jnp.sum(p, axis=-1, keepdims=True) + jnp.exp(sink - m)
            o_h = _dot((p / denom).astype(BF16), jnp.where(half_kv, v_dup[g], 0.0).astype(BF16))
            acc = o_h if acc is None else acc + o_h
        outs.append(acc)
    o_ref[...] = jnp.concatenate(outs, axis=-1).astype(o_ref.dtype)


def _swa_attention(q, k, v, sinks, batch):
    rows = q.shape[0]
    nblk = rows // batch // SWA_BLOCK
    cur = lambda n: pl.BlockSpec((SWA_BLOCK, n), lambda b, i: (b * nblk + i, 0))
    prev = lambda n: pl.BlockSpec((SWA_BLOCK, n), lambda b, i: (b * nblk + jnp.maximum(i - 1, 0), 0))
    return pl.pallas_call(
        _swa_kernel,
        grid=(batch, nblk),
        in_specs=[pl.BlockSpec(memory_space=pltpu.SMEM), cur(SWA_Q_DIM),
                  prev(SWA_KV_DIM), cur(SWA_KV_DIM), prev(SWA_KV_DIM), cur(SWA_KV_DIM)],
        out_specs=cur(SWA_Q_DIM),
        out_shape=jax.ShapeDtypeStruct((rows, SWA_Q_DIM), BF16),
        compiler_params=_params(2),
        name="swa_attention",
    )(sinks.astype(F32), q, k, k, v, v)


def _rglru_kernel(gate_ref, xr_ref, cw_ref, cb_ref, wg_ref, bg_ref, lam_ref, o_ref, h_ref, xcat_ref):
    t = RG_CHUNK
    halo = SUBLANES

    @pl.when(pl.program_id(1) == 0)
    def _():
        h_ref[...] = jnp.zeros_like(h_ref)
        xcat_ref[0:halo, :] = jnp.zeros((halo, RG_WIDTH), F32)

    xcat_ref[halo:halo + t, :] = xr_ref[...]
    cw = cw_ref[...]
    xc = cb_ref[...] + cw[3:4, :] * xcat_ref[halo:halo + t, :]
    for k in range(1, RG_CONV):
        xc = xc + cw[3 - k:4 - k, :] * xcat_ref[halo - k:halo - k + t, :]
    xcat_ref[0:halo, :] = xcat_ref[t:t + halo, :]

    gates = _dot(xc.astype(BF16), wg_ref[...]) + bg_ref[...]
    r = _sigmoid(gates[:, :RG_WIDTH])
    i = _sigmoid(gates[:, RG_WIDTH:])
    log_a = (-RG_C) * r * _softplus(-lam_ref[...])
    a = jnp.exp(log_a)
    th = jnp.tanh(log_a)
    u = jnp.sqrt(jnp.maximum(-2.0 * th / (1.0 - th), 0.0)) * (i * xc)

    row = lax.broadcasted_iota(jnp.int32, (t, LANES), 0)
    gelu_gate = _gelu_tanh(gate_ref[...])
    outs = []
    for c in range(RG_WIDTH // LANES):
        sl = slice(c * LANES, (c + 1) * LANES)
        a_c, u_c = a[:, sl], u[:, sl]
        d = 1
        while d < t:
            keep = row >= d
            u_c = u_c + a_c * jnp.where(keep, pltpu.roll(u_c, d, axis=0), 0.0)
            a_c = a_c * jnp.where(keep, pltpu.roll(a_c, d, axis=0), 1.0)
            d *= 2
        h_c = u_c + a_c * h_ref[0:1, sl]
        h_ref[0:1, sl] = h_c[t - 1:t, :]
        outs.append(h_c)
    o_ref[...] = (jnp.concatenate(outs, axis=-1) * gelu_gate).astype(o_ref.dtype)


def _block_diag(w):
    nb, d, _ = w.shape
    eye = jnp.eye(nb, dtype=w.dtype)
    return (eye[:, None, :, None] * w[:, :, None, :]).reshape(nb * d, nb * d)


def _rglru_mixer(gate, xr, conv_w, conv_b, wa, ba, wx, bx, lam, batch):
    rows = gate.shape[0]
    nchunk = rows // batch // RG_CHUNK
    row_spec = pl.BlockSpec((RG_CHUNK, RG_WIDTH), lambda b, c: (b * nchunk + c, 0))
    w_gates = jnp.concatenate([_block_diag(wa.astype(F32)), _block_diag(wx.astype(F32))], axis=1).astype(BF16)
    b_gates = jnp.concatenate([ba, bx]).astype(F32).reshape(1, -1)
    return pl.pallas_call(
        _rglru_kernel,
        grid=(batch, nchunk),
        in_specs=[row_spec, row_spec, _const_spec((RG_CONV, RG_WIDTH)), _const_spec((1, RG_WIDTH)),
                  _const_spec((RG_WIDTH, 2 * RG_WIDTH)), _const_spec((1, 2 * RG_WIDTH)),
                  _const_spec((1, RG_WIDTH))],
        out_specs=row_spec,
        out_shape=jax.ShapeDtypeStruct((rows, RG_WIDTH), BF16),
        scratch_shapes=[pltpu.VMEM((SUBLANES, RG_WIDTH), F32),
                        pltpu.VMEM((RG_CHUNK + SUBLANES, RG_WIDTH), F32)],
        compiler_params=_params(2),
        name="rglru_mixer",
    )(gate, xr, conv_w.astype(F32), conv_b.astype(F32).reshape(1, -1), w_gates, b_gates,
      lam.astype(F32).reshape(1, -1))


def _outproj_kernel(ya_ref, yb_ref, x_ref, w_ref, g_ref, b_ref, o_ref):
    half = ya_ref.shape[1]
    m = _dot(ya_ref[...], w_ref[0:half, :]) + _dot(yb_ref[...], w_ref[half:2 * half, :])
    o_ref[...] = _layer_norm(ALPHA * x_ref[...] + m, g_ref[...], b_ref[...])


def _outproj_ln(ya, yb, x, w_out, ln_g, ln_b):
    rows, d = x.shape
    half = ya.shape[1]
    return pl.pallas_call(
        _outproj_kernel,
        grid=(rows // ROW_TILE,),
        in_specs=[pl.BlockSpec((ROW_TILE, half), lambda i: (i, 0)), pl.BlockSpec((ROW_TILE, half), lambda i: (i, 0)),
                  pl.BlockSpec((ROW_TILE, d), lambda i: (i, 0)), _const_spec((2 * half, d)),
                  _const_spec((1, d)), _const_spec((1, d))],
        out_specs=pl.BlockSpec((ROW_TILE, d), lambda i: (i, 0)),
        out_shape=jax.ShapeDtypeStruct((rows, d), F32),
        compiler_params=_params(1),
        name="outproj_ln",
    )(ya, yb, x, w_out.astype(BF16), ln_g.astype(F32).reshape(1, -1), ln_b.astype(F32).reshape(1, -1))


def _ffn_kernel(x_ref, wup_ref, cw_ref, cb_ref, wdown_ref, g_ref, b_ref, o_ref, carry_ref, h_ref, act_ref):
    tm = ROW_TILE
    halo = SUBLANES
    fc = FFN_COL_CHUNK

    @pl.when(pl.program_id(1) == 0)
    def _():
        carry_ref[...] = jnp.zeros_like(carry_ref)

    x = x_ref[...]
    xb = x.astype(BF16)
    for c in range(FFN_DIM // fc):
        conv = []
        for part in range(2):
            col = part * FFN_DIM + c * fc
            hb = h_ref.at[c % 2, part]
            hb[0:halo, :] = carry_ref[:, col:col + fc]
            hb[halo:halo + tm, :] = _dot(xb, wup_ref[:, col:col + fc])
            carry_ref[:, col:col + fc] = hb[tm:tm + halo, :]
            y = cb_ref[:, col:col + fc] + cw_ref[2:3, col:col + fc] * hb[halo:halo + tm, :]
            for k in range(1, FFN_CONV):
                y = y + cw_ref[2 - k:3 - k, col:col + fc] * hb[halo - k:halo - k + tm, :]
            conv.append(y)
        act_ref[:, c * fc:(c + 1) * fc] = (_silu(conv[0]) * conv[1]).astype(BF16)
    f = _dot(act_ref[...], wdown_ref[...])
    o_ref[...] = _layer_norm(ALPHA * x + f, g_ref[...], b_ref[...])


def _ffn_ln(x, w_up, conv_w, conv_b, w_down, ln_g, ln_b, batch):
    rows, d = x.shape
    ntile = rows // batch // ROW_TILE
    row_spec = pl.BlockSpec((ROW_TILE, d), lambda b, j: (b * ntile + j, 0))
    return pl.pallas_call(
        _ffn_kernel,
        grid=(batch, ntile),
        in_specs=[row_spec, _const_spec((d, 2 * FFN_DIM)), _const_spec((FFN_CONV, 2 * FFN_DIM)),
                  _const_spec((1, 2 * FFN_DIM)), _const_spec((FFN_DIM, d)), _const_spec((1, d)), _const_spec((1, d))],
        out_specs=row_spec,
        out_shape=jax.ShapeDtypeStruct((rows, d), F32),
        scratch_shapes=[pltpu.VMEM((SUBLANES, 2 * FFN_DIM), F32),
                        pltpu.VMEM((2, 2, ROW_TILE + SUBLANES, FFN_COL_CHUNK), F32),
                        pltpu.VMEM((ROW_TILE, FFN_DIM), BF16)],
        compiler_params=_params(2),
        name="ffn_ln",
    )(x, w_up.astype(BF16), conv_w.astype(F32), conv_b.astype(F32).reshape(1, -1), w_down.astype(BF16),
      ln_g.astype(F32).reshape(1, -1), ln_b.astype(F32).reshape(1, -1))


AB_SPLITS = (SSD_D_INNER, SSD_CONV_DIM, LANES, HG_WIDTH, HG_WIDTH, HG_WIDTH, HG_WIDTH)
CD_SPLITS = (SWA_Q_DIM, SWA_KV_DIM, SWA_KV_DIM, RG_WIDTH, RG_WIDTH)


def _ab_weight(w_in):
    a = SSD_D_INNER + SSD_CONV_DIM
    dt_cols = jnp.pad(w_in[:, a:a + SSD_HEADS], ((0, 0), (0, LANES - SSD_HEADS)))
    return jnp.concatenate([w_in[:, :a], dt_cols, w_in[:, a + SSD_HEADS:]], axis=1).astype(BF16)


def kernel(x, ab_w_in, ssd_conv_w, ssd_conv_b, ssd_dt_bias, ssd_a_log, ssd_d, ssd_norm_w, hg_lower, hg_norm_w, ab_w_out, cd_w_in, swa_sinks, rg_conv_w, rg_conv_b, rg_wa, rg_ba, rg_wx, rg_bx, rg_lambda, cd_w_out, ffn_w_up, ffn_conv_w, ffn_conv_b, ffn_w_down, ln_g, ln_b):
    batch, seq, d = x.shape
    assert d == D_MODEL and seq % ROW_TILE == 0
    xr = x.reshape(batch * seq, d).astype(F32)
    for layer in range(DEPTH):
        j = layer // 2
        if layer % 2 == 0:
            z, xbc, dt_raw, hq, hf, hi, hg = _inproj(xr, _ab_weight(ab_w_in[j]), AB_SPLITS, (F32,) * len(AB_SPLITS))
            ya = _ssd_mixer(z, xbc, dt_raw, ssd_conv_w[j], ssd_conv_b[j], ssd_dt_bias[j], ssd_a_log[j], ssd_d[j],
                            ssd_norm_w[j], batch)
            yb = _hgrn2_mixer(hq, hf, hi, hg, hg_lower, hg_norm_w[j], j, batch)
            w_out = ab_w_out[j]
        else:
            q, k, v, gate, xg = _inproj(xr, cd_w_in[j].astype(BF16), CD_SPLITS, (F32,) * len(CD_SPLITS))
            ya = _swa_attention(q, k, v, swa_sinks[j], batch)
            yb = _rglru_mixer(gate, xg, rg_conv_w[j], rg_conv_b[j], rg_wa[j], rg_ba[j], rg_wx[j], rg_bx[j],
                              rg_lambda[j], batch)
            w_out = cd_w_out[j]
        xr = _outproj_ln(ya, yb, xr, w_out, ln_g[layer, 0], ln_b[layer, 0])
        xr = _ffn_ln(xr, ffn_w_up[layer], ffn_conv_w[layer], ffn_conv_b[layer], ffn_w_down[layer],
                     ln_g[layer, 1], ln_b[layer, 1], batch)
    return xr.reshape(batch, seq, d).astype(x.dtype)
```

```python
import functools

import jax
import jax.numpy as jnp
import numpy as np
from jax import lax
from jax.experimental import pallas as pl
from jax.experimental.pallas import tpu as pltpu

F32 = jnp.float32
BF16 = jnp.bfloat16
HIGHEST = lax.Precision.HIGHEST

LANES = 128
SUBLANES = 8
VMEM_LIMIT_BYTES = 56 * 1024 * 1024

D_MODEL = 1024
DEPTH = 4
SSD_HEADS = 8
SSD_HEAD_DIM = 64
SSD_D_INNER = SSD_HEADS * SSD_HEAD_DIM
SSD_GROUPS = 2
SSD_D_STATE = 64
SSD_CONV = 4
SSD_CONV_DIM = SSD_D_INNER + 2 * SSD_GROUPS * SSD_D_STATE
SSD_CHUNK = 128
HG_HEADS = 4
HG_KEY_DIM = 128
HG_VAL_DIM = 128
HG_WIDTH = HG_HEADS * HG_KEY_DIM
HG_CHUNK = 64
HG_SAFE_DECAY = 160.0
SWA_Q_HEADS = 8
SWA_KV_HEADS = 2
SWA_HEAD_DIM = 64
SWA_BLOCK = 128
SWA_Q_DIM = SWA_Q_HEADS * SWA_HEAD_DIM
SWA_KV_DIM = SWA_KV_HEADS * SWA_HEAD_DIM
RG_WIDTH = 512
RG_BLOCKS = 8
RG_BLOCK_DIM = RG_WIDTH // RG_BLOCKS
RG_CONV = 4
RG_C = 8.0
RG_CHUNK = 128
FFN_DIM = 2816
FFN_CONV = 3
FFN_COL_CHUNK = 256
ROW_TILE = 512
LN_EPS = 1e-5
RMS_EPS = 1e-6
MASK_VALUE = -1e9
ALPHA = (2 * DEPTH) ** 0.25


def _params(n_axes):
    return pltpu.CompilerParams(dimension_semantics=("arbitrary",) * n_axes,
                                vmem_limit_bytes=VMEM_LIMIT_BYTES)


def _const_spec(shape):
    return pl.BlockSpec(shape, lambda *_: (0,) * len(shape), pipeline_mode=pl.Buffered(1))


def _sigmoid(x):
    return 1.0 / (1.0 + jnp.exp(-x))


def _silu(x):
    return x * _sigmoid(x)


def _softplus(x):
    return jnp.maximum(x, 0.0) + jnp.log1p(jnp.exp(-jnp.abs(x)))


def _gelu_tanh(x):
    return 0.5 * x * (1.0 + jnp.tanh(np.sqrt(2.0 / np.pi).astype(np.float32) * (x + 0.044715 * (x * x * x))))


def _layer_norm(r, g, b):
    mu = jnp.mean(r, axis=-1, keepdims=True)
    d = r - mu
    var = jnp.mean(d * d, axis=-1, keepdims=True)
    return d * lax.rsqrt(var + LN_EPS) * g + b


def _dot(a, b):
    return jnp.dot(a, b, preferred_element_type=F32)


def _dot_exact(a, b):
    return jnp.dot(a, b, preferred_element_type=F32, precision=HIGHEST)


def _dot_nt(a, b):
    return lax.dot_general(a, b, (((1,), (1,)), ((), ())), preferred_element_type=F32)


def _dot_tn(a, b):
    return lax.dot_general(a, b, (((0,), (0,)), ((), ())), preferred_element_type=F32)


def _lower_tri(n):
    row = lax.broadcasted_iota(jnp.int32, (n, n), 0)
    col = lax.broadcasted_iota(jnp.int32, (n, n), 1)
    return col <= row


def _inproj_kernel(x_ref, w_ref, *o_refs, splits):
    xb = x_ref[...].astype(BF16)
    off = 0
    for o_ref, n in zip(o_refs, splits):
        o_ref[...] = _dot(xb, w_ref[:, off:off + n]).astype(o_ref.dtype)
        off += n


def _inproj(x, w, splits, dtypes):
    rows, d = x.shape
    n_total = sum(splits)
    return pl.pallas_call(
        functools.partial(_inproj_kernel, splits=splits),
        grid=(rows // ROW_TILE,),
        in_specs=[pl.BlockSpec((ROW_TILE, d), lambda i: (i, 0)), _const_spec((d, n_total))],
        out_specs=[pl.BlockSpec((ROW_TILE, n), lambda i: (i, 0)) for n in splits],
        out_shape=[jax.ShapeDtypeStruct((rows, n), dt) for n, dt in zip(splits, dtypes)],
        compiler_params=_params(1),
        name="inproj",
    )(x, w)


def _seq_spec(batch, t, n):
    return pl.BlockSpec((batch, t, n), lambda c: (0, c, 0))


def _causal_conv(x_new, xcat_ref, b, cw, cb, width, t):
    halo = SUBLANES
    xcat_ref[b, halo:halo + t, :] = x_new
    y = cb + cw[width - 1:width, :] * x_new
    for k in range(1, width):
        y = y + cw[width - 1 - k:width - k, :] * xcat_ref[b, halo - k:halo - k + t, :]
    xcat_ref[b, 0:halo, :] = xcat_ref[b, t:t + halo, :]
    return y


def _ssd_kernel(z_ref, xbc_ref, dt_ref, cw_ref, cb_ref, dtb_ref, alog_ref, dskip_ref, nw_ref, expand_ref,
                o_ref, state_ref, xcat_ref):
    t = SSD_CHUNK
    batch = z_ref.shape[0]

    @pl.when(pl.program_id(0) == 0)
    def _():
        state_ref[...] = jnp.zeros_like(state_ref)
        xcat_ref[:, 0:SUBLANES, :] = jnp.zeros((batch, SUBLANES, SSD_CONV_DIM), F32)

    cw = cw_ref[...]
    cb = cb_ref[...]
    neg_a = -jnp.exp(alog_ref[...])
    expand = expand_ref[...]
    tri = _lower_tri(t)
    tri_f = tri.astype(F32)
    lane = lax.broadcasted_iota(jnp.int32, (t, LANES), 1)
    lo_half = lane < SSD_D_STATE
    sub = lax.broadcasted_iota(jnp.int32, (LANES, LANES), 0)
    heads_per_group = SSD_HEADS // SSD_GROUPS
    gw = SSD_D_INNER // SSD_GROUPS

    for b in range(batch):
        xa = _silu(_causal_conv(xbc_ref[b], xcat_ref, b, cw, cb, SSD_CONV, t))
        xs = xa[:, :SSD_D_INNER]
        bm = xa[:, SSD_D_INNER:SSD_D_INNER + LANES]
        cm = xa[:, SSD_D_INNER + LANES:SSD_D_INNER + 2 * LANES]

        dt = _softplus(dt_ref[b] + dtb_ref[...])
        cs = _dot_exact(tri_f, dt * neg_a)
        cs_row = cs.T
        dt_f = _dot_exact(dt, expand)
        cs_f = _dot_exact(cs, expand)
        cs_last = cs_f[t - 1:t, :]
        xdt = xs * dt_f
        ecs = jnp.exp(cs_f)
        w_state = (jnp.exp(cs_last - cs_f) * xdt).astype(BF16)
        dec_last = jnp.exp(cs_last)

        bm_t = bm.T.astype(BF16)
        cm_g = [jnp.where(lo_half, cm, 0.0).astype(BF16), jnp.where(lo_half, 0.0, cm).astype(BF16)]
        scores_g = [_dot(c, bm_t) for c in cm_g]

        y_pairs = []
        for pair in range(SSD_HEADS // 2):
            g = (2 * pair) // heads_per_group
            sl = slice(LANES * pair, LANES * (pair + 1))
            x_pair = xdt[:, sl]
            y_pair = None
            for j in range(2):
                h = 2 * pair + j
                x_h = jnp.where(lo_half if j == 0 else jnp.logical_not(lo_half), x_pair, 0.0).astype(BF16)
                diff = cs[:, h:h + 1] - cs_row[h:h + 1, :]
                decay = jnp.where(tri, jnp.exp(jnp.where(tri, diff, 0.0)), 0.0)
                y_h = _dot((scores_g[g] * decay).astype(BF16), x_h)
                y_pair = y_h if y_pair is None else y_pair + y_h
            state = state_ref[b, pair]
            y_pair = y_pair + _dot(cm_g[g], state.astype(BF16)) * ecs[:, sl]
            upd = _dot(bm_t, w_state[:, sl])
            in_group = (sub >= g * SSD_D_STATE) & (sub < (g + 1) * SSD_D_STATE)
            state_ref[b, pair] = jnp.where(in_group, upd, 0.0) + dec_last[:, sl] * state
            y_pairs.append(y_pair)

        y = jnp.concatenate(y_pairs, axis=-1) + dskip_ref[...] * xs
        y = y * _silu(z_ref[b])
        outs = []
        for g in range(SSD_GROUPS):
            yg = y[:, g * gw:(g + 1) * gw]
            ms = jnp.mean(yg * yg, axis=-1, keepdims=True)
            outs.append(yg * lax.rsqrt(ms + RMS_EPS) * nw_ref[:, g * gw:(g + 1) * gw])
        o_ref[b] = jnp.concatenate(outs, axis=-1).astype(o_ref.dtype)


def _ssd_mixer(z, xbc, dt_raw, conv_w, conv_b, dt_bias, a_log, d_skip, norm_w):
    batch, seq, _ = z.shape
    pad_heads = lambda v: jnp.pad(v.astype(F32), (0, LANES - SSD_HEADS)).reshape(1, LANES)
    expand = (jnp.arange(LANES)[:, None] == (jnp.arange(SSD_D_INNER) // SSD_HEAD_DIM)[None, :]).astype(F32)
    return pl.pallas_call(
        _ssd_kernel,
        grid=(seq // SSD_CHUNK,),
        in_specs=[_seq_spec(batch, SSD_CHUNK, SSD_D_INNER), _seq_spec(batch, SSD_CHUNK, SSD_CONV_DIM),
                  _seq_spec(batch, SSD_CHUNK, LANES),
                  _const_spec((SSD_CONV, SSD_CONV_DIM)), _const_spec((1, SSD_CONV_DIM)),
                  _const_spec((1, LANES)), _const_spec((1, LANES)),
                  _const_spec((1, SSD_D_INNER)), _const_spec((1, SSD_D_INNER)),
                  _const_spec((LANES, SSD_D_INNER))],
        out_specs=_seq_spec(batch, SSD_CHUNK, SSD_D_INNER),
        out_shape=jax.ShapeDtypeStruct((batch, seq, SSD_D_INNER), BF16),
        scratch_shapes=[pltpu.VMEM((batch, SSD_HEADS // 2, LANES, LANES), F32),
                        pltpu.VMEM((batch, SSD_CHUNK + SUBLANES, SSD_CONV_DIM), F32)],
        compiler_params=_params(1),
        name="ssd_mixer",
    )(z, xbc, dt_raw, conv_w.astype(F32), conv_b.astype(F32).reshape(1, -1), pad_heads(dt_bias), pad_heads(a_log),
      jnp.repeat(d_skip.astype(F32), SSD_HEAD_DIM).reshape(1, -1), norm_w.astype(F32).reshape(1, -1), expand)


def _hgrn2_kernel(hq_ref, hf_ref, hi_ref, hg_ref, lower_ref, nw_ref, o_ref, state_ref, attn_ref, *, layer):
    t = HG_CHUNK
    batch = hq_ref.shape[0]

    @pl.when(pl.program_id(0) == 0)
    def _():
        state_ref[...] = jnp.zeros_like(state_ref)

    low = lower_ref[...]
    ex = jnp.exp(low - jnp.max(low, axis=0, keepdims=True))
    sm = ex / jnp.sum(ex, axis=0, keepdims=True)
    lb = jnp.clip(jnp.sum(sm[:layer + 1, :], axis=0, keepdims=True) - sm[0:1, :], 0.0, 1.0)
    tri = _lower_tri(t)
    tri_f = tri.astype(F32)
    head_slices = [slice(h * HG_KEY_DIM, (h + 1) * HG_KEY_DIM) for h in range(HG_HEADS)]

    per_seq = []
    worst = None
    for b in range(batch):
        q = _silu(hq_ref[b])
        fx = hf_ref[b]
        log_f = jnp.log(lb + (1.0 - lb) * _sigmoid(fx))
        k = (1.0 - lb) * _sigmoid(-fx)
        bc = _dot_exact(tri_f, log_f)
        b_last = bc[t - 1:t, :]
        mid = 0.5 * b_last
        q_mid = (q * jnp.exp(bc - mid)).astype(BF16)
        k_mid = (k * jnp.exp(mid - bc)).astype(BF16)
        for h, sl in enumerate(head_slices):
            attn_ref[b, h] = _dot_nt(q_mid[:, sl], k_mid[:, sl])
        per_seq.append((q, k, bc, b_last))
        decay = jnp.max(-b_last)
        worst = decay if worst is None else jnp.maximum(worst, decay)

    @pl.when(jnp.logical_not(worst <= HG_SAFE_DECAY))
    def _():
        col_id = lax.broadcasted_iota(jnp.int32, (t, t), 1)
        row_id = lax.broadcasted_iota(jnp.int32, (t, HG_KEY_DIM), 0)
        for b in range(batch):
            q, k, bc, _ = per_seq[b]
            for h, sl in enumerate(head_slices):

                def column(s, acc, q_h=q[:, sl], k_h=k[:, sl], bc_h=bc[:, sl]):
                    k_s = jnp.sum(jnp.where(row_id == s, k_h, 0.0), axis=0, keepdims=True)
                    bc_s = jnp.sum(jnp.where(row_id == s, bc_h, 0.0), axis=0, keepdims=True)
                    w = q_h * k_s * jnp.exp(jnp.minimum(bc_h - bc_s, 0.0))
                    return jnp.where(col_id == s, jnp.sum(w, axis=-1, keepdims=True), acc)

                attn_ref[b, h] = lax.fori_loop(0, t, column, jnp.zeros((t, t), F32))

    for b in range(batch):
        q, k, bc, b_last = per_seq[b]
        vb = hi_ref[b].astype(BF16)
        q_state = (q * jnp.exp(bc)).astype(BF16)
        k_state = (k * jnp.exp(b_last - bc)).astype(BF16)
        dec_last = jnp.exp(b_last)
        outs = []
        for h, sl in enumerate(head_slices):
            attn = jnp.where(tri, attn_ref[b, h], 0.0).astype(BF16)
            state = state_ref[b, h]
            o = _dot(attn, vb[:, sl]) + _dot_nt(q_state[:, sl], state.astype(BF16))
            state_ref[b, h] = state * dec_last[:, sl] + _dot_tn(vb[:, sl], k_state[:, sl])
            ms = jnp.mean(o * o, axis=-1, keepdims=True)
            outs.append(o * lax.rsqrt(ms + RMS_EPS) * nw_ref[...])
        o_ref[b] = (jnp.concatenate(outs, axis=-1) * _silu(hg_ref[b])).astype(o_ref.dtype)


def _hgrn2_mixer(hq, hf, hi, hg, hg_lower, norm_w, layer):
    batch, seq, _ = hq.shape
    spec = _seq_spec(batch, HG_CHUNK, HG_WIDTH)
    return pl.pallas_call(
        functools.partial(_hgrn2_kernel, layer=layer),
        grid=(seq // HG_CHUNK,),
        in_specs=[spec, spec, spec, spec, _const_spec(hg_lower.shape), _const_spec((1, HG_VAL_DIM))],
        out_specs=spec,
        out_shape=jax.ShapeDtypeStruct((batch, seq, HG_WIDTH), BF16),
        scratch_shapes=[pltpu.VMEM((batch, HG_HEADS, HG_VAL_DIM, HG_KEY_DIM), F32),
                        pltpu.VMEM((batch, HG_HEADS, HG_CHUNK, HG_CHUNK), F32)],
        compiler_params=_params(1),
        name="hgrn2_mixer",
    )(hq, hf, hi, hg, hg_lower.astype(F32), norm_w.astype(F32).reshape(1, -1))


def _swa_kernel(sinks_ref, q_ref, kp_ref, kc_ref, vp_ref, vc_ref, o_ref):
    t = SWA_BLOCK
    batch = q_ref.shape[0]
    group = SWA_Q_HEADS // SWA_KV_HEADS
    rows = group * t

    lane_kv = lax.broadcasted_iota(jnp.int32, (2 * t, LANES), 1)
    lo_kv = lane_kv < SWA_HEAD_DIM
    lane_q = lax.broadcasted_iota(jnp.int32, (t, LANES), 1)
    lo_q = lane_q < SWA_HEAD_DIM
    row = lax.broadcasted_iota(jnp.int32, (rows, 2 * t), 0) & (t - 1)
    col = lax.broadcasted_iota(jnp.int32, (rows, 2 * t), 1)
    first_block = pl.program_id(0) == 0
    visible = (col > row) & (col <= row + t) & ((col >= t) | jnp.logical_not(first_block))
    head_of_row = lax.broadcasted_iota(jnp.int32, (rows, 1), 0) // t

    for b in range(batch):
        q = q_ref[b] * (SWA_HEAD_DIM ** -0.5)
        kcat = jnp.concatenate([kp_ref[b], kc_ref[b]], axis=0)
        vcat = jnp.concatenate([vp_ref[b], vc_ref[b]], axis=0)
        k_swap = pltpu.roll(kcat, SWA_HEAD_DIM, axis=1)
        v_swap = pltpu.roll(vcat, SWA_HEAD_DIM, axis=1)
        outs = []
        for g in range(SWA_KV_HEADS):
            k_dup = (jnp.where(lo_kv, kcat, k_swap) if g == 0 else jnp.where(lo_kv, k_swap, kcat)).astype(BF16)
            v_dup = (jnp.where(lo_kv, vcat, v_swap) if g == 0 else jnp.where(lo_kv, v_swap, vcat)).astype(BF16)
            pieces = []
            sink = jnp.zeros((rows, 1), F32)
            for hh in range(group):
                head = g * group + hh
                q_pair = q[:, (head // 2) * LANES:(head // 2 + 1) * LANES]
                pieces.append(jnp.where(lo_q if head % 2 == 0 else jnp.logical_not(lo_q), q_pair, 0.0))
                sink = jnp.where(head_of_row == hh, sinks_ref[head], sink)
            s = _dot_nt(jnp.concatenate(pieces, axis=0).astype(BF16), k_dup)
            s = jnp.where(visible, s, MASK_VALUE)
            m = jnp.maximum(jnp.max(s, axis=-1, keepdims=True), sink)
            p = jnp.exp(s - m)
            denom = jnp.sum(p, axis=-1, keepdims=True) + jnp.exp(sink - m)
            o = _dot((p / denom).astype(BF16), v_dup)
            for pp in range(group // 2):
                outs.append(jnp.where(lo_q, o[2 * pp * t:(2 * pp + 1) * t, :], o[(2 * pp + 1) * t:(2 * pp + 2) * t, :]))
        o_ref[b] = jnp.concatenate(outs, axis=-1).astype(o_ref.dtype)


def _swa_attention(q, k, v, sinks):
    batch, seq, _ = q.shape
    cur = lambda n: pl.BlockSpec((batch, SWA_BLOCK, n), lambda i: (0, i, 0))
    prev = lambda n: pl.BlockSpec((batch, SWA_BLOCK, n), lambda i: (0, jnp.maximum(i - 1, 0), 0))
    return pl.pallas_call(
        _swa_kernel,
        grid=(seq // SWA_BLOCK,),
        in_specs=[pl.BlockSpec(memory_space=pltpu.SMEM), cur(SWA_Q_DIM),
                  prev(SWA_KV_DIM), cur(SWA_KV_DIM), prev(SWA_KV_DIM), cur(SWA_KV_DIM)],
        out_specs=cur(SWA_Q_DIM),
        out_shape=jax.ShapeDtypeStruct((batch, seq, SWA_Q_DIM), BF16),
        compiler_params=_params(1),
        name="swa_attention",
    )(sinks.astype(F32), q, k, k, v, v)


def _rglru_kernel(gate_ref, xr_ref, cw_ref, cb_ref, wg_ref, bg_ref, lam_ref, o_ref, h_ref, xcat_ref):
    t = RG_CHUNK
    batch = gate_ref.shape[0]

    @pl.when(pl.program_id(0) == 0)
    def _():
        h_ref[...] = jnp.zeros_like(h_ref)
        xcat_ref[:, 0:SUBLANES, :] = jnp.zeros((batch, SUBLANES, RG_WIDTH), F32)

    cw = cw_ref[...]
    cb = cb_ref[...]
    decay_rate = (-RG_C) * _softplus(-lam_ref[...])
    row = lax.broadcasted_iota(jnp.int32, (t, LANES), 0)
    for b in range(batch):
        xc = _causal_conv(xr_ref[b], xcat_ref, b, cw, cb, RG_CONV, t)
        gates = _dot(xc.astype(BF16), wg_ref[...]) + bg_ref[...]
        log_a = decay_rate * _sigmoid(gates[:, :RG_WIDTH])
        a = jnp.exp(log_a)
        th = jnp.tanh(log_a)
        u = jnp.sqrt(jnp.maximum(-2.0 * th / (1.0 - th), 0.0)) * (_sigmoid(gates[:, RG_WIDTH:]) * xc)
        outs = []
        for c in range(RG_WIDTH // LANES):
            sl = slice(c * LANES, (c + 1) * LANES)
            a_c, u_c = a[:, sl], u[:, sl]
            d = 1
            while d < t:
                keep = row >= d
                u_c = u_c + a_c * jnp.where(keep, pltpu.roll(u_c, d, axis=0), 0.0)
                a_c = a_c * jnp.where(keep, pltpu.roll(a_c, d, axis=0), 1.0)
                d *= 2
            h_c = u_c + a_c * h_ref[b, 0:1, sl]
            h_ref[b, 0:1, sl] = h_c[t - 1:t, :]
            outs.append(h_c)
        o_ref[b] = (jnp.concatenate(outs, axis=-1) * _gelu_tanh(gate_ref[b])).astype(o_ref.dtype)


def _block_diag(w):
    nb, d, _ = w.shape
    eye = jnp.eye(nb, dtype=w.dtype)
    return (eye[:, None, :, None] * w[:, :, None, :]).reshape(nb * d, nb * d)


def _rglru_mixer(gate, xr, conv_w, conv_b, wa, ba, wx, bx, lam):
    batch, seq, _ = gate.shape
    spec = _seq_spec(batch, RG_CHUNK, RG_WIDTH)
    w_gates = jnp.concatenate([_block_diag(wa.astype(F32)), _block_diag(wx.astype(F32))], axis=1).astype(BF16)
    b_gates = jnp.concatenate([ba, bx]).astype(F32).reshape(1, -1)
    return pl.pallas_call(
        _rglru_kernel,
        grid=(seq // RG_CHUNK,),
        in_specs=[spec, spec, _const_spec((RG_CONV, RG_WIDTH)), _const_spec((1, RG_WIDTH)),
                  _const_spec((RG_WIDTH, 2 * RG_WIDTH)), _const_spec((1, 2 * RG_WIDTH)),
                  _const_spec((1, RG_WIDTH))],
        out_specs=spec,
        out_shape=jax.ShapeDtypeStruct((batch, seq, RG_WIDTH), BF16),
        scratch_shapes=[pltpu.VMEM((batch, SUBLANES, RG_WIDTH), F32),
                        pltpu.VMEM((batch, RG_CHUNK + SUBLANES, RG_WIDTH), F32)],
        compiler_params=_params(1),
        name="rglru_mixer",
    )(gate, xr, conv_w.astype(F32), conv_b.astype(F32).reshape(1, -1), w_gates, b_gates,
      lam.astype(F32).reshape(1, -1))


def _outproj_kernel(ya_ref, yb_ref, x_ref, w_ref, g_ref, b_ref, o_ref):
    half = ya_ref.shape[1]
    m = _dot(ya_ref[...], w_ref[0:half, :]) + _dot(yb_ref[...], w_ref[half:2 * half, :])
    o_ref[...] = _layer_norm(ALPHA * x_ref[...] + m, g_ref[...], b_ref[...])


def _outproj_ln(ya, yb, x, w_out, ln_g, ln_b):
    rows, d = x.shape
    half = ya.shape[1]
    return pl.pallas_call(
        _outproj_kernel,
        grid=(rows // ROW_TILE,),
        in_specs=[pl.BlockSpec((ROW_TILE, half), lambda i: (i, 0)), pl.BlockSpec((ROW_TILE, half), lambda i: (i, 0)),
                  pl.BlockSpec((ROW_TILE, d), lambda i: (i, 0)), _const_spec((2 * half, d)),
                  _const_spec((1, d)), _const_spec((1, d))],
        out_specs=pl.BlockSpec((ROW_TILE, d), lambda i: (i, 0)),
        out_shape=jax.ShapeDtypeStruct((rows, d), F32),
        compiler_params=_params(1),
        name="outproj_ln",
    )(ya, yb, x, w_out.astype(BF16), ln_g.astype(F32).reshape(1, -1), ln_b.astype(F32).reshape(1, -1))


def _ffn_kernel(x_ref, wup_ref, cw_ref, cb_ref, wdown_ref, g_ref, b_ref, o_ref, carry_ref, h_ref, act_ref):
    tm = ROW_TILE
    halo = SUBLANES
    fc = FFN_COL_CHUNK

    @pl.when(pl.program_id(1) == 0)
    def _():
        carry_ref[...] = jnp.zeros_like(carry_ref)

    x = x_ref[...]
    xb = x.astype(BF16)
    for c in range(FFN_DIM // fc):
        conv = []
        for part in range(2):
            col = part * FFN_DIM + c * fc
            hb = h_ref.at[c % 2, part]
            hb[0:halo, :] = carry_ref[:, col:col + fc]
            hb[halo:halo + tm, :] = _dot(xb, wup_ref[:, col:col + fc])
            carry_ref[:, col:col + fc] = hb[tm:tm + halo, :]
            y = cb_ref[:, col:col + fc] + cw_ref[2:3, col:col + fc] * hb[halo:halo + tm, :]
            for k in range(1, FFN_CONV):
                y = y + cw_ref[2 - k:3 - k, col:col + fc] * hb[halo - k:halo - k + tm, :]
            conv.append(y)
        act_ref[:, c * fc:(c + 1) * fc] = (_silu(conv[0]) * conv[1]).astype(BF16)
    f = _dot(act_ref[...], wdown_ref[...])
    o_ref[...] = _layer_norm(ALPHA * x + f, g_ref[...], b_ref[...])


def _ffn_ln(x, w_up, conv_w, conv_b, w_down, ln_g, ln_b, batch):
    rows, d = x.shape
    ntile = rows // batch // ROW_TILE
    row_spec = pl.BlockSpec((ROW_TILE, d), lambda b, j: (b * ntile + j, 0))
    return pl.pallas_call(
        _ffn_kernel,
        grid=(batch, ntile),
        in_specs=[row_spec, _const_spec((d, 2 * FFN_DIM)), _const_spec((FFN_CONV, 2 * FFN_DIM)),
                  _const_spec((1, 2 * FFN_DIM)), _const_spec((FFN_DIM, d)), _const_spec((1, d)), _const_spec((1, d))],
        out_specs=row_spec,
        out_shape=jax.ShapeDtypeStruct((rows, d), F32),
        scratch_shapes=[pltpu.VMEM((SUBLANES, 2 * FFN_DIM), F32),
                        pltpu.VMEM((2, 2, ROW_TILE + SUBLANES, FFN_COL_CHUNK), F32),
                        pltpu.VMEM((ROW_TILE, FFN_DIM), BF16)],
        compiler_params=_params(2),
        name="ffn_ln",
    )(x, w_up.astype(BF16), conv_w.astype(F32), conv_b.astype(F32).reshape(1, -1), w_down.astype(BF16),
      ln_g.astype(F32).reshape(1, -1), ln_b.astype(F32).reshape(1, -1))


AB_SPLITS = (SSD_D_INNER, SSD_CONV_DIM, LANES, HG_WIDTH, HG_WIDTH, HG_WIDTH, HG_WIDTH)
CD_SPLITS = (SWA_Q_DIM, SWA_KV_DIM, SWA_KV_DIM, RG_WIDTH, RG_WIDTH)


def _ab_weight(w_in):
    a = SSD_D_INNER + SSD_CONV_DIM
    dt_cols = jnp.pad(w_in[:, a:a + SSD_HEADS], ((0, 0), (0, LANES - SSD_HEADS)))
    return jnp.concatenate([w_in[:, :a], dt_cols, w_in[:, a + SSD_HEADS:]], axis=1).astype(BF16)


def kernel(x, ab_w_in, ssd_conv_w, ssd_conv_b, ssd_dt_bias, ssd_a_log, ssd_d, ssd_norm_w, hg_lower, hg_norm_w, ab_w_out, cd_w_in, swa_sinks, rg_conv_w, rg_conv_b, rg_wa, rg_ba, rg_wx, rg_bx, rg_lambda, cd_w_out, ffn_w_up, ffn_conv_w, ffn_conv_b, ffn_w_down, ln_g, ln_b):
    batch, seq, d = x.shape
    assert d == D_MODEL and seq % ROW_TILE == 0
    rows = batch * seq
    seq_view = lambda arrs: [a.reshape(batch, seq, a.shape[-1]) for a in arrs]
    xr = x.reshape(rows, d).astype(F32)
    for layer in range(DEPTH):
        j = layer // 2
        if layer % 2 == 0:
            z, xbc, dt_raw, hq, hf, hi, hg = seq_view(
                _inproj(xr, _ab_weight(ab_w_in[j]), AB_SPLITS, (F32,) * len(AB_SPLITS)))
            ya = _ssd_mixer(z, xbc, dt_raw, ssd_conv_w[j], ssd_conv_b[j], ssd_dt_bias[j], ssd_a_log[j], ssd_d[j],
                            ssd_norm_w[j])
            yb = _hgrn2_mixer(hq, hf, hi, hg, hg_lower, hg_norm_w[j], j)
            w_out = ab_w_out[j]
        else:
            q, k, v, gate, xg = seq_view(_inproj(xr, cd_w_in[j].astype(BF16), CD_SPLITS, (F32,) * len(CD_SPLITS)))
            ya = _swa_attention(q, k, v, swa_sinks[j])
            yb = _rglru_mixer(gate, xg, rg_conv_w[j], rg_conv_b[j], rg_wa[j], rg_ba[j], rg_wx[j], rg_bx[j],
                              rg_lambda[j])
            w_out = cd_w_out[j]
        xr = _outproj_ln(ya.reshape(rows, -1), yb.reshape(rows, -1), xr, w_out, ln_g[layer, 0], ln_b[layer, 0])
        xr = _ffn_ln(xr, ffn_w_up[layer], ffn_conv_w[layer], ffn_conv_b[layer], ffn_w_down[layer],
                     ln_g[layer, 1], ln_b[layer, 1], batch)
    return xr.reshape(batch, seq, d).astype(x.dtype)
```

```python
import functools

import jax
import jax.numpy as jnp
import numpy as np
from jax import lax
from jax.experimental import pallas as pl
from jax.experimental.pallas import tpu as pltpu

F32 = jnp.float32
BF16 = jnp.bfloat16

LANES = 128
SUBLANES = 8
VMEM_LIMIT_BYTES = 56 * 1024 * 1024

D_MODEL = 1024
DEPTH = 4
SSD_HEADS = 8
SSD_HEAD_DIM = 64
SSD_D_INNER = SSD_HEADS * SSD_HEAD_DIM
SSD_GROUPS = 2
SSD_D_STATE = 64
SSD_CONV = 4
SSD_CONV_DIM = SSD_D_INNER + 2 * SSD_GROUPS * SSD_D_STATE
SSD_CHUNK = 128
HG_HEADS = 4
HG_KEY_DIM = 128
HG_VAL_DIM = 128
HG_WIDTH = HG_HEADS * HG_KEY_DIM
HG_CHUNK = 64
HG_SAFE_DECAY = 160.0
SWA_Q_HEADS = 8
SWA_KV_HEADS = 2
SWA_HEAD_DIM = 64
SWA_BLOCK = 128
SWA_Q_DIM = SWA_Q_HEADS * SWA_HEAD_DIM
SWA_KV_DIM = SWA_KV_HEADS * SWA_HEAD_DIM
RG_WIDTH = 512
RG_BLOCKS = 8
RG_BLOCK_DIM = RG_WIDTH // RG_BLOCKS
RG_CONV = 4
RG_C = 8.0
RG_CHUNK = 128
FFN_DIM = 2816
FFN_CONV = 3
FFN_COL_CHUNK = 256
ROW_TILE = 512
LN_EPS = 1e-5
RMS_EPS = 1e-6
MASK_VALUE = -1e9
ALPHA = (2 * DEPTH) ** 0.25


def _params(n_axes):
    return pltpu.CompilerParams(dimension_semantics=("arbitrary",) * n_axes,
                                vmem_limit_bytes=VMEM_LIMIT_BYTES)


def _const_spec(shape):
    return pl.BlockSpec(shape, lambda *_: (0,) * len(shape), pipeline_mode=pl.Buffered(1))


def _sigmoid(x):
    return 1.0 / (1.0 + jnp.exp(-x))


def _silu(x):
    return x * _sigmoid(x)


def _softplus(x):
    return jnp.maximum(x, 0.0) + jnp.log1p(jnp.exp(-jnp.abs(x)))


def _gelu_tanh(x):
    return 0.5 * x * (1.0 + jnp.tanh(np.sqrt(2.0 / np.pi).astype(np.float32) * (x + 0.044715 * (x * x * x))))


def _layer_norm(r, g, b):
    mu = jnp.mean(r, axis=-1, keepdims=True)
    d = r - mu
    var = jnp.mean(d * d, axis=-1, keepdims=True)
    return d * lax.rsqrt(var + LN_EPS) * g + b


def _dot(a, b):
    return jnp.dot(a, b, preferred_element_type=F32)


def _split3(x):
    x1 = x.astype(BF16)
    r = x - x1.astype(F32)
    x2 = r.astype(BF16)
    return x1, x2, (r - x2.astype(F32)).astype(BF16)


def _dot_nt(a, b):
    return lax.dot_general(a, b, (((1,), (1,)), ((), ())), preferred_element_type=F32)


def _dot_tn(a, b):
    return lax.dot_general(a, b, (((0,), (0,)), ((), ())), preferred_element_type=F32)


def _lower_tri(n):
    row = lax.broadcasted_iota(jnp.int32, (n, n), 0)
    col = lax.broadcasted_iota(jnp.int32, (n, n), 1)
    return col <= row


def _inproj_kernel(x_ref, w_ref, *o_refs, splits):
    xb = x_ref[...].astype(BF16)
    off = 0
    for o_ref, n in zip(o_refs, splits):
        o_ref[...] = _dot(xb, w_ref[:, off:off + n]).astype(o_ref.dtype)
        off += n


def _inproj(x, w, splits, dtypes):
    rows, d = x.shape
    n_total = sum(splits)
    return pl.pallas_call(
        functools.partial(_inproj_kernel, splits=splits),
        grid=(rows // ROW_TILE,),
        in_specs=[pl.BlockSpec((ROW_TILE, d), lambda i: (i, 0)), _const_spec((d, n_total))],
        out_specs=[pl.BlockSpec((ROW_TILE, n), lambda i: (i, 0)) for n in splits],
        out_shape=[jax.ShapeDtypeStruct((rows, n), dt) for n, dt in zip(splits, dtypes)],
        compiler_params=_params(1),
        name="inproj",
    )(x, w)


def _seq_spec(batch, t, n):
    return pl.BlockSpec((batch, t, n), lambda c: (0, c, 0))


def _causal_conv(x_new, xcat_ref, b, cw, cb, width, t):
    halo = SUBLANES
    ys = []
    for s in range(x_new.shape[1] // LANES):
        sl = slice(s * LANES, (s + 1) * LANES)
        xcat_ref[b, s, halo:halo + t, :] = x_new[:, sl]
        y = cb[:, sl] + cw[width - 1:width, sl] * x_new[:, sl]
        for k in range(1, width):
            y = y + cw[width - 1 - k:width - k, sl] * xcat_ref[b, s, halo - k:halo - k + t, :]
        xcat_ref[b, s, 0:halo, :] = xcat_ref[b, s, t:t + halo, :]
        ys.append(y)
    return jnp.concatenate(ys, axis=-1)


def _ssd_kernel(z_ref, xbc_ref, dt_ref, cw_ref, cb_ref, dtb_ref, alog_ref, dskip_ref, nw_ref, expand_ref,
                o_ref, state_ref, xcat_ref):
    t = SSD_CHUNK
    batch = z_ref.shape[0]

    @pl.when(pl.program_id(0) == 0)
    def _():
        state_ref[...] = jnp.zeros_like(state_ref)
        xcat_ref[:, :, 0:SUBLANES, :] = jnp.zeros((batch, SSD_CONV_DIM // LANES, SUBLANES, LANES), F32)

    cw = cw_ref[...]
    cb = cb_ref[...]
    neg_a = -jnp.exp(alog_ref[...])
    expand = expand_ref[...]
    tri = _lower_tri(t)
    tri3 = jnp.concatenate([jnp.where(tri, 1.0, 0.0).astype(BF16)] * 3, axis=1)
    lane = lax.broadcasted_iota(jnp.int32, (t, LANES), 1)
    lo_half = lane < SSD_D_STATE
    sub = lax.broadcasted_iota(jnp.int32, (LANES, LANES), 0)
    heads_per_group = SSD_HEADS // SSD_GROUPS
    gw = SSD_D_INNER // SSD_GROUPS

    for b in range(batch):
        xa = _silu(_causal_conv(xbc_ref[b], xcat_ref, b, cw, cb, SSD_CONV, t))
        xs = xa[:, :SSD_D_INNER]
        bm = xa[:, SSD_D_INNER:SSD_D_INNER + LANES]
        cm = xa[:, SSD_D_INNER + LANES:SSD_D_INNER + 2 * LANES]

        dt = _softplus(dt_ref[b] + dtb_ref[...])
        cs = _dot(tri3, jnp.concatenate(_split3(dt * neg_a), axis=0))
        cs_row = cs.T
        per_channel = _dot(jnp.concatenate(_split3(jnp.concatenate([dt, cs], axis=0)), axis=1), expand)
        dt_f = per_channel[:t, :]
        cs_f = per_channel[t:, :]
        cs_last = cs_f[t - 1:t, :]
        xdt = xs * dt_f
        ecs = jnp.exp(cs_f)
        w_state = (jnp.exp(cs_last - cs_f) * xdt).astype(BF16)
        dec_last = jnp.exp(cs_last)

        bm_t = bm.T.astype(BF16)
        cm_g = [jnp.where(lo_half, cm, 0.0).astype(BF16), jnp.where(lo_half, 0.0, cm).astype(BF16)]
        scores_g = [_dot(c, bm_t) for c in cm_g]

        y_pairs = []
        for pair in range(SSD_HEADS // 2):
            g = (2 * pair) // heads_per_group
            sl = slice(LANES * pair, LANES * (pair + 1))
            x_pair = xdt[:, sl]
            y_pair = None
            for j in range(2):
                h = 2 * pair + j
                x_h = jnp.where(lo_half if j == 0 else jnp.logical_not(lo_half), x_pair, 0.0).astype(BF16)
                diff = cs[:, h:h + 1] - cs_row[h:h + 1, :]
                decay = jnp.where(tri, jnp.exp(jnp.where(tri, diff, 0.0)), 0.0)
                y_h = _dot((scores_g[g] * decay).astype(BF16), x_h)
                y_pair = y_h if y_pair is None else y_pair + y_h
            state = state_ref[b, pair]
            y_pair = y_pair + _dot(cm_g[g], state.astype(BF16)) * ecs[:, sl]
            upd = _dot(bm_t, w_state[:, sl])
            in_group = (sub >= g * SSD_D_STATE) & (sub < (g + 1) * SSD_D_STATE)
            state_ref[b, pair] = jnp.where(in_group, upd, 0.0) + dec_last[:, sl] * state
            y_pairs.append(y_pair)

        y = jnp.concatenate(y_pairs, axis=-1) + dskip_ref[...] * xs
        y = y * _silu(z_ref[b])
        outs = []
        for g in range(SSD_GROUPS):
            yg = y[:, g * gw:(g + 1) * gw]
            ms = jnp.mean(yg * yg, axis=-1, keepdims=True)
            outs.append(yg * lax.rsqrt(ms + RMS_EPS) * nw_ref[:, g * gw:(g + 1) * gw])
        o_ref[b] = jnp.concatenate(outs, axis=-1).astype(o_ref.dtype)


def _ssd_mixer(z, xbc, dt_raw, conv_w, conv_b, dt_bias, a_log, d_skip, norm_w):
    batch, seq, _ = z.shape
    pad_heads = lambda v: jnp.pad(v.astype(F32), (0, LANES - SSD_HEADS)).reshape(1, LANES)
    expand = (jnp.arange(LANES)[:, None] == (jnp.arange(SSD_D_INNER) // SSD_HEAD_DIM)[None, :]).astype(BF16)
    expand = jnp.concatenate([expand] * 3, axis=0)
    return pl.pallas_call(
        _ssd_kernel,
        grid=(seq // SSD_CHUNK,),
        in_specs=[_seq_spec(batch, SSD_CHUNK, SSD_D_INNER), _seq_spec(batch, SSD_CHUNK, SSD_CONV_DIM),
                  _seq_spec(batch, SSD_CHUNK, LANES),
                  _const_spec((SSD_CONV, SSD_CONV_DIM)), _const_spec((1, SSD_CONV_DIM)),
                  _const_spec((1, LANES)), _const_spec((1, LANES)),
                  _const_spec((1, SSD_D_INNER)), _const_spec((1, SSD_D_INNER)),
                  _const_spec((3 * LANES, SSD_D_INNER))],
        out_specs=_seq_spec(batch, SSD_CHUNK, SSD_D_INNER),
        out_shape=jax.ShapeDtypeStruct((batch, seq, SSD_D_INNER), BF16),
        scratch_shapes=[pltpu.VMEM((batch, SSD_HEADS // 2, LANES, LANES), F32),
                        pltpu.VMEM((batch, SSD_CONV_DIM // LANES, SSD_CHUNK + SUBLANES, LANES), F32)],
        compiler_params=_params(1),
        name="ssd_mixer",
    )(z, xbc, dt_raw, conv_w.astype(F32), conv_b.astype(F32).reshape(1, -1), pad_heads(dt_bias), pad_heads(a_log),
      jnp.repeat(d_skip.astype(F32), SSD_HEAD_DIM).reshape(1, -1), norm_w.astype(F32).reshape(1, -1), expand)


def _hgrn2_kernel(hq_ref, hf_ref, hi_ref, hg_ref, lower_ref, nw_ref, o_ref, state_ref, attn_ref, *, layer):
    t = HG_CHUNK
    batch = hq_ref.shape[0]
    rows = batch * t
    stack = lambda ref: jnp.concatenate([ref[b] for b in range(batch)], axis=0)

    @pl.when(pl.program_id(0) == 0)
    def _():
        state_ref[...] = jnp.zeros_like(state_ref)

    low = lower_ref[...]
    ex = jnp.exp(low - jnp.max(low, axis=0, keepdims=True))
    sm = ex / jnp.sum(ex, axis=0, keepdims=True)
    lb = jnp.clip(jnp.sum(sm[:layer + 1, :], axis=0, keepdims=True) - sm[0:1, :], 0.0, 1.0)

    row = lax.broadcasted_iota(jnp.int32, (rows, rows), 0)
    col = lax.broadcasted_iota(jnp.int32, (rows, rows), 1)
    causal = (col <= row) & ((row & -t) == (col & -t))
    causal_b = jnp.where(causal, 1.0, 0.0).astype(BF16)
    head_slices = [slice(h * HG_KEY_DIM, (h + 1) * HG_KEY_DIM) for h in range(HG_HEADS)]

    q = _silu(stack(hq_ref))
    fx = stack(hf_ref)
    e = jnp.exp(-jnp.abs(fx))
    big = 1.0 / (1.0 + e)
    small = e * big
    pos = fx >= 0.0
    log_f = jnp.log(lb + (1.0 - lb) * jnp.where(pos, big, small))
    k = (1.0 - lb) * jnp.where(pos, small, big)
    bc = _dot(jnp.concatenate([causal_b] * 3, axis=1), jnp.concatenate(_split3(log_f), axis=0))
    last_rows = [bc[(b + 1) * t - 1:(b + 1) * t, :] for b in range(batch)]
    b_last = jnp.concatenate([jnp.broadcast_to(r, (t, HG_WIDTH)) for r in last_rows], axis=0)

    mid = 0.5 * b_last
    q_mid = (q * jnp.exp(bc - mid)).astype(BF16)
    k_mid = (k * jnp.exp(mid - bc)).astype(BF16)
    for h, sl in enumerate(head_slices):
        attn_ref[h] = _dot_nt(q_mid[:, sl], k_mid[:, sl])

    @pl.when(jnp.logical_not(jnp.max(-jnp.concatenate(last_rows, axis=0)) <= HG_SAFE_DECAY))
    def _():
        col_id = lax.broadcasted_iota(jnp.int32, (rows, rows), 1)
        row_id = lax.broadcasted_iota(jnp.int32, (rows, HG_KEY_DIM), 0)
        for h, sl in enumerate(head_slices):

            def column(s, acc, q_h=q[:, sl], k_h=k[:, sl], bc_h=bc[:, sl]):
                k_s = jnp.sum(jnp.where(row_id == s, k_h, 0.0), axis=0, keepdims=True)
                bc_s = jnp.sum(jnp.where(row_id == s, bc_h, 0.0), axis=0, keepdims=True)
                w = q_h * k_s * jnp.exp(jnp.minimum(bc_h - bc_s, 0.0))
                return jnp.where(col_id == s, jnp.sum(w, axis=-1, keepdims=True), acc)

            attn_ref[h] = lax.fori_loop(0, rows, column, jnp.zeros((rows, rows), F32))

    vb = stack(hi_ref)
    q_state = (q * jnp.exp(bc)).astype(BF16)
    k_state = (k * jnp.exp(b_last - bc)).astype(BF16)
    outs = []
    for h, sl in enumerate(head_slices):
        o = _dot(jnp.where(causal, attn_ref[h], 0.0).astype(BF16), vb[:, sl])
        o_inter = []
        for b in range(batch):
            rs = slice(b * t, (b + 1) * t)
            state = state_ref[b, h]
            o_inter.append(_dot_nt(q_state[rs, sl], state.astype(BF16)))
            state_ref[b, h] = (state * jnp.exp(last_rows[b][:, sl])
                               + _dot_tn(vb[rs, sl], k_state[rs, sl]))
        o = o + jnp.concatenate(o_inter, axis=0)
        ms = jnp.mean(o * o, axis=-1, keepdims=True)
        outs.append(o * lax.rsqrt(ms + RMS_EPS) * nw_ref[...])
    y = jnp.concatenate(outs, axis=-1) * _silu(stack(hg_ref))
    for b in range(batch):
        o_ref[b] = y[b * t:(b + 1) * t, :].astype(o_ref.dtype)


def _hgrn2_mixer(hq, hf, hi, hg, hg_lower, norm_w, layer):
    batch, seq, _ = hq.shape
    spec = _seq_spec(batch, HG_CHUNK, HG_WIDTH)
    return pl.pallas_call(
        functools.partial(_hgrn2_kernel, layer=layer),
        grid=(seq // HG_CHUNK,),
        in_specs=[spec, spec, spec, spec, _const_spec(hg_lower.shape), _const_spec((1, HG_VAL_DIM))],
        out_specs=spec,
        out_shape=jax.ShapeDtypeStruct((batch, seq, HG_WIDTH), BF16),
        scratch_shapes=[pltpu.VMEM((batch, HG_HEADS, HG_VAL_DIM, HG_KEY_DIM), F32),
                        pltpu.VMEM((HG_HEADS, batch * HG_CHUNK, batch * HG_CHUNK), F32)],
        compiler_params=_params(1),
        name="hgrn2_mixer",
    )(hq, hf, hi, hg, hg_lower.astype(F32), norm_w.astype(F32).reshape(1, -1))


def _swa_kernel(sinks_ref, q_ref, kp_ref, kc_ref, vp_ref, vc_ref, o_ref):
    t = SWA_BLOCK
    batch = q_ref.shape[0]
    group = SWA_Q_HEADS // SWA_KV_HEADS
    rows = group * t

    lane_kv = lax.broadcasted_iota(jnp.int32, (2 * t, LANES), 1)
    lo_kv = lane_kv < SWA_HEAD_DIM
    lane_q = lax.broadcasted_iota(jnp.int32, (t, LANES), 1)
    lo_q = lane_q < SWA_HEAD_DIM
    row = lax.broadcasted_iota(jnp.int32, (rows, 2 * t), 0) & (t - 1)
    col = lax.broadcasted_iota(jnp.int32, (rows, 2 * t), 1)
    first_block = pl.program_id(0) == 0
    visible = (col > row) & (col <= row + t) & ((col >= t) | jnp.logical_not(first_block))
    head_of_row = lax.broadcasted_iota(jnp.int32, (rows, 1), 0) // t

    for b in range(batch):
        q = q_ref[b].astype(F32) * (SWA_HEAD_DIM ** -0.5)
        kcat = jnp.concatenate([kp_ref[b], kc_ref[b]], axis=0).astype(F32)
        vcat = jnp.concatenate([vp_ref[b], vc_ref[b]], axis=0).astype(F32)
        k_swap = pltpu.roll(kcat, SWA_HEAD_DIM, axis=1)
        v_swap = pltpu.roll(vcat, SWA_HEAD_DIM, axis=1)
        outs = []
        for g in range(SWA_KV_HEADS):
            k_dup = (jnp.where(lo_kv, kcat, k_swap) if g == 0 else jnp.where(lo_kv, k_swap, kcat)).astype(BF16)
            v_dup = (jnp.where(lo_kv, vcat, v_swap) if g == 0 else jnp.where(lo_kv, v_swap, vcat)).astype(BF16)
            pieces = []
            sink = jnp.zeros((rows, 1), F32)
            for hh in range(group):
                head = g * group + hh
                q_pair = q[:, (head // 2) * LANES:(head // 2 + 1) * LANES]
                pieces.append(jnp.where(lo_q if head % 2 == 0 else jnp.logical_not(lo_q), q_pair, 0.0))
                sink = jnp.where(head_of_row == hh, sinks_ref[head], sink)
            s = _dot_nt(jnp.concatenate(pieces, axis=0).astype(BF16), k_dup)
            s = jnp.where(visible, s, MASK_VALUE)
            m = jnp.maximum(jnp.max(s, axis=-1, keepdims=True), sink)
            p = jnp.exp(s - m)
            denom = jnp.sum(p, axis=-1, keepdims=True) + jnp.exp(sink - m)
            o = _dot((p / denom).astype(BF16), v_dup)
            for pp in range(group // 2):
                outs.append(jnp.where(lo_q, o[2 * pp * t:(2 * pp + 1) * t, :], o[(2 * pp + 1) * t:(2 * pp + 2) * t, :]))
        o_ref[b] = jnp.concatenate(outs, axis=-1).astype(o_ref.dtype)


def _swa_attention(q, k, v, sinks):
    batch, seq, _ = q.shape
    cur = lambda n: pl.BlockSpec((batch, SWA_BLOCK, n), lambda i: (0, i, 0))
    prev = lambda n: pl.BlockSpec((batch, SWA_BLOCK, n), lambda i: (0, jnp.maximum(i - 1, 0), 0))
    return pl.pallas_call(
        _swa_kernel,
        grid=(seq // SWA_BLOCK,),
        in_specs=[pl.BlockSpec(memory_space=pltpu.SMEM), cur(SWA_Q_DIM),
                  prev(SWA_KV_DIM), cur(SWA_KV_DIM), prev(SWA_KV_DIM), cur(SWA_KV_DIM)],
        out_specs=cur(SWA_Q_DIM),
        out_shape=jax.ShapeDtypeStruct((batch, seq, SWA_Q_DIM), BF16),
        compiler_params=_params(1),
        name="swa_attention",
    )(sinks.astype(F32), q, k, k, v, v)


def _rglru_kernel(gate_ref, xr_ref, cw_ref, cb_ref, wg_ref, bg_ref, lam_ref, o_ref, h_ref, xcat_ref):
    t = RG_CHUNK
    batch = gate_ref.shape[0]
    groups = t // SUBLANES

    @pl.when(pl.program_id(0) == 0)
    def _():
        h_ref[...] = jnp.zeros_like(h_ref)
        xcat_ref[:, :, 0:SUBLANES, :] = jnp.zeros((batch, RG_WIDTH // LANES, SUBLANES, LANES), F32)

    cw = cw_ref[...]
    cb = cb_ref[...]
    decay_rate = (-RG_C) * _softplus(-lam_ref[...])
    sub = lax.broadcasted_iota(jnp.int32, (groups, SUBLANES, LANES), 1)
    for b in range(batch):
        xc = _causal_conv(xr_ref[b], xcat_ref, b, cw, cb, RG_CONV, t)
        gates = _dot(xc.astype(BF16), wg_ref[...]) + bg_ref[...]
        log_a = decay_rate * _sigmoid(gates[:, :RG_WIDTH])
        a = jnp.exp(log_a)
        th = jnp.tanh(log_a)
        u = jnp.sqrt(jnp.maximum(-2.0 * th / (1.0 - th), 0.0)) * (_sigmoid(gates[:, RG_WIDTH:]) * xc)
        outs = []
        for c in range(RG_WIDTH // LANES):
            sl = slice(c * LANES, (c + 1) * LANES)
            a_c = a[:, sl].reshape(groups, SUBLANES, LANES)
            u_c = u[:, sl].reshape(groups, SUBLANES, LANES)
            d = 1
            while d < SUBLANES:
                keep = sub >= d
                u_c = u_c + a_c * jnp.where(keep, pltpu.roll(u_c, d, axis=1), 0.0)
                a_c = a_c * jnp.where(keep, pltpu.roll(a_c, d, axis=1), 1.0)
                d *= 2
            h_prev = h_ref[b, 0:1, sl]
            h_groups = []
            for g in range(groups):
                h_g = u_c[g] + a_c[g] * h_prev
                h_prev = h_g[SUBLANES - 1:SUBLANES, :]
                h_groups.append(h_g)
            h_ref[b, 0:1, sl] = h_prev
            outs.append(jnp.concatenate(h_groups, axis=0))
        o_ref[b] = (jnp.concatenate(outs, axis=-1) * _gelu_tanh(gate_ref[b])).astype(o_ref.dtype)


def _block_diag(w):
    nb, d, _ = w.shape
    eye = jnp.eye(nb, dtype=w.dtype)
    return (eye[:, None, :, None] * w[:, :, None, :]).reshape(nb * d, nb * d)


def _rglru_mixer(gate, xr, conv_w, conv_b, wa, ba, wx, bx, lam):
    batch, seq, _ = gate.shape
    spec = _seq_spec(batch, RG_CHUNK, RG_WIDTH)
    w_gates = jnp.concatenate([_block_diag(wa.astype(F32)), _block_diag(wx.astype(F32))], axis=1).astype(BF16)
    b_gates = jnp.concatenate([ba, bx]).astype(F32).reshape(1, -1)
    return pl.pallas_call(
        _rglru_kernel,
        grid=(seq // RG_CHUNK,),
        in_specs=[spec, spec, _const_spec((RG_CONV, RG_WIDTH)), _const_spec((1, RG_WIDTH)),
                  _const_spec((RG_WIDTH, 2 * RG_WIDTH)), _const_spec((1, 2 * RG_WIDTH)),
                  _const_spec((1, RG_WIDTH))],
        out_specs=spec,
        out_shape=jax.ShapeDtypeStruct((batch, seq, RG_WIDTH), BF16),
        scratch_shapes=[pltpu.VMEM((batch, SUBLANES, RG_WIDTH), F32),
                        pltpu.VMEM((batch, RG_WIDTH // LANES, RG_CHUNK + SUBLANES, LANES), F32)],
        compiler_params=_params(1),
        name="rglru_mixer",
    )(gate, xr, conv_w.astype(F32), conv_b.astype(F32).reshape(1, -1), w_gates, b_gates,
      lam.astype(F32).reshape(1, -1))


def _outproj_kernel(ya_ref, yb_ref, x_ref, w_ref, g_ref, b_ref, o_ref):
    half = ya_ref.shape[1]
    m = _dot(ya_ref[...], w_ref[0:half, :]) + _dot(yb_ref[...], w_ref[half:2 * half, :])
    o_ref[...] = _layer_norm(ALPHA * x_ref[...] + m, g_ref[...], b_ref[...])


def _outproj_ln(ya, yb, x, w_out, ln_g, ln_b):
    rows, d = x.shape
    half = ya.shape[1]
    return pl.pallas_call(
        _outproj_kernel,
        grid=(rows // ROW_TILE,),
        in_specs=[pl.BlockSpec((ROW_TILE, half), lambda i: (i, 0)), pl.BlockSpec((ROW_TILE, half), lambda i: (i, 0)),
                  pl.BlockSpec((ROW_TILE, d), lambda i: (i, 0)), _const_spec((2 * half, d)),
                  _const_spec((1, d)), _const_spec((1, d))],
        out_specs=pl.BlockSpec((ROW_TILE, d), lambda i: (i, 0)),
        out_shape=jax.ShapeDtypeStruct((rows, d), F32),
        compiler_params=_params(1),
        name="outproj_ln",
    )(ya, yb, x, w_out.astype(BF16), ln_g.astype(F32).reshape(1, -1), ln_b.astype(F32).reshape(1, -1))


def _ffn_up(xb, wup_ref, cw_ref, cb_ref, carry_ref, h_ref, act_ref):
    tm = ROW_TILE
    halo = SUBLANES
    fc = FFN_COL_CHUNK
    for c in range(FFN_DIM // fc):
        h = [_dot(xb, wup_ref[:, part * FFN_DIM + c * fc:part * FFN_DIM + (c + 1) * fc]) for part in range(2)]
        for s in range(fc // LANES):
            conv = []
            for part in range(2):
                col = part * FFN_DIM + c * fc + s * LANES
                hb = h_ref.at[c % 2, part, s]
                hb[0:halo, :] = carry_ref[:, col:col + LANES]
                hb[halo:halo + tm, :] = h[part][:, s * LANES:(s + 1) * LANES]
                carry_ref[:, col:col + LANES] = hb[tm:tm + halo, :]
                y = cb_ref[:, col:col + LANES] + cw_ref[2:3, col:col + LANES] * hb[halo:halo + tm, :]
                for k in range(1, FFN_CONV):
                    y = y + cw_ref[2 - k:3 - k, col:col + LANES] * hb[halo - k:halo - k + tm, :]
                conv.append(y)
            act_col = c * fc + s * LANES
            act_ref[:, act_col:act_col + LANES] = (_silu(conv[0]) * conv[1]).astype(BF16)


def _ffn_kernel(x_ref, wup_ref, cw_ref, cb_ref, wdown_ref, g_ref, b_ref, o_ref, carry_ref, h_ref, act_ref):
    @pl.when(pl.program_id(1) == 0)
    def _():
        carry_ref[...] = jnp.zeros_like(carry_ref)

    x = x_ref[...]
    _ffn_up(x.astype(BF16), wup_ref, cw_ref, cb_ref, carry_ref, h_ref, act_ref)
    f = _dot(act_ref[...], wdown_ref[...])
    o_ref[...] = _layer_norm(ALPHA * x + f, g_ref[...], b_ref[...])


def _ffn_ln(x, w_up, conv_w, conv_b, w_down, ln_g, ln_b, batch):
    rows, d = x.shape
    ntile = rows // batch // ROW_TILE
    row_spec = pl.BlockSpec((ROW_TILE, d), lambda b, j: (b * ntile + j, 0))
    return pl.pallas_call(
        _ffn_kernel,
        grid=(batch, ntile),
        in_specs=[row_spec, _const_spec((d, 2 * FFN_DIM)), _const_spec((FFN_CONV, 2 * FFN_DIM)),
                  _const_spec((1, 2 * FFN_DIM)), _const_spec((FFN_DIM, d)), _const_spec((1, d)), _const_spec((1, d))],
        out_specs=row_spec,
        out_shape=jax.ShapeDtypeStruct((rows, d), F32),
        scratch_shapes=[pltpu.VMEM((SUBLANES, 2 * FFN_DIM), F32),
                        pltpu.VMEM((2, 2, FFN_COL_CHUNK // LANES, ROW_TILE + SUBLANES, LANES), F32),
                        pltpu.VMEM((ROW_TILE, FFN_DIM), BF16)],
        compiler_params=_params(2),
        name="ffn_ln",
    )(x, w_up.astype(BF16), conv_w.astype(F32), conv_b.astype(F32).reshape(1, -1), w_down.astype(BF16),
      ln_g.astype(F32).reshape(1, -1), ln_b.astype(F32).reshape(1, -1))


AB_SPLITS = (SSD_D_INNER, SSD_CONV_DIM, LANES, HG_WIDTH, HG_WIDTH, HG_WIDTH, HG_WIDTH)
CD_SPLITS = (SWA_Q_DIM, SWA_KV_DIM, SWA_KV_DIM, RG_WIDTH, RG_WIDTH)
AB_DTYPES = (F32, F32, F32, F32, F32, BF16, F32)
CD_DTYPES = (BF16, BF16, BF16, F32, F32)


def _ab_weight(w_in):
    a = SSD_D_INNER + SSD_CONV_DIM
    dt_cols = jnp.pad(w_in[:, a:a + SSD_HEADS], ((0, 0), (0, LANES - SSD_HEADS)))
    return jnp.concatenate([w_in[:, :a], dt_cols, w_in[:, a + SSD_HEADS:]], axis=1).astype(BF16)


def kernel(x, ab_w_in, ssd_conv_w, ssd_conv_b, ssd_dt_bias, ssd_a_log, ssd_d, ssd_norm_w, hg_lower, hg_norm_w, ab_w_out, cd_w_in, swa_sinks, rg_conv_w, rg_conv_b, rg_wa, rg_ba, rg_wx, rg_bx, rg_lambda, cd_w_out, ffn_w_up, ffn_conv_w, ffn_conv_b, ffn_w_down, ln_g, ln_b):
    batch, seq, d = x.shape
    assert d == D_MODEL and seq % ROW_TILE == 0
    rows = batch * seq
    seq_view = lambda arrs: [a.reshape(batch, seq, a.shape[-1]) for a in arrs]
    xr = x.reshape(rows, d).astype(F32)
    for layer in range(DEPTH):
        j = layer // 2
        if layer % 2 == 0:
            z, xbc, dt_raw, hq, hf, hi, hg = seq_view(_inproj(xr, _ab_weight(ab_w_in[j]), AB_SPLITS, AB_DTYPES))
            ya = _ssd_mixer(z, xbc, dt_raw, ssd_conv_w[j], ssd_conv_b[j], ssd_dt_bias[j], ssd_a_log[j], ssd_d[j],
                            ssd_norm_w[j])
            yb = _hgrn2_mixer(hq, hf, hi, hg, hg_lower, hg_norm_w[j], j)
            w_out = ab_w_out[j]
        else:
            q, k, v, gate, xg = seq_view(_inproj(xr, cd_w_in[j].astype(BF16), CD_SPLITS, CD_DTYPES))
            ya = _swa_attention(q, k, v, swa_sinks[j])
            yb = _rglru_mixer(gate, xg, rg_conv_w[j], rg_conv_b[j], rg_wa[j], rg_ba[j], rg_wx[j], rg_bx[j],
                              rg_lambda[j])
            w_out = cd_w_out[j]
        xr = _outproj_ln(ya.reshape(rows, -1), yb.reshape(rows, -1), xr, w_out, ln_g[layer, 0], ln_b[layer, 0])
        xr = _ffn_ln(xr, ffn_w_up[layer], ffn_conv_w[layer], ffn_conv_b[layer], ffn_w_down[layer],
                     ln_g[layer, 1], ln_b[layer, 1], batch)
    return xr.reshape(batch, seq, d).astype(x.dtype)
```

```python
import functools

import jax
import jax.numpy as jnp
import numpy as np
from jax import lax
from jax.experimental import pallas as pl
from jax.experimental.pallas import tpu as pltpu

F32 = jnp.float32
BF16 = jnp.bfloat16

LANES = 128
SUBLANES = 8
VMEM_LIMIT_BYTES = 56 * 1024 * 1024

D_MODEL = 1024
DEPTH = 4
SSD_HEADS = 8
SSD_HEAD_DIM = 64
SSD_D_INNER = SSD_HEADS * SSD_HEAD_DIM
SSD_GROUPS = 2
SSD_D_STATE = 64
SSD_CONV = 4
SSD_CONV_DIM = SSD_D_INNER + 2 * SSD_GROUPS * SSD_D_STATE
SSD_CHUNK = 128
HG_HEADS = 4
HG_KEY_DIM = 128
HG_VAL_DIM = 128
HG_WIDTH = HG_HEADS * HG_KEY_DIM
HG_CHUNK = 64
HG_SAFE_DECAY = 160.0
SWA_Q_HEADS = 8
SWA_KV_HEADS = 2
SWA_HEAD_DIM = 64
SWA_BLOCK = 128
SWA_Q_DIM = SWA_Q_HEADS * SWA_HEAD_DIM
SWA_KV_DIM = SWA_KV_HEADS * SWA_HEAD_DIM
RG_WIDTH = 512
RG_BLOCKS = 8
RG_BLOCK_DIM = RG_WIDTH // RG_BLOCKS
RG_CONV = 4
RG_C = 8.0
RG_CHUNK = 128
FFN_DIM = 2816
FFN_CONV = 3
FFN_COL_CHUNK = 256
ROW_TILE = 512
LN_EPS = 1e-5
RMS_EPS = 1e-6
MASK_VALUE = -1e9
ALPHA = (2 * DEPTH) ** 0.25


def _params(n_axes):
    return pltpu.CompilerParams(dimension_semantics=("arbitrary",) * n_axes,
                                vmem_limit_bytes=VMEM_LIMIT_BYTES)


def _const_spec(shape):
    return pl.BlockSpec(shape, lambda *_: (0,) * len(shape), pipeline_mode=pl.Buffered(1))


def _layer_spec(shape, layer):
    return pl.BlockSpec((None,) + tuple(shape), lambda *_: (layer,) + (0,) * len(shape),
                        pipeline_mode=pl.Buffered(1))


def _sigmoid(x):
    return 1.0 / (1.0 + jnp.exp(-x))


def _silu(x):
    return x * _sigmoid(x)


def _softplus(x):
    return jnp.maximum(x, 0.0) + jnp.log1p(jnp.exp(-jnp.abs(x)))


def _gelu_tanh(x):
    return 0.5 * x * (1.0 + jnp.tanh(np.sqrt(2.0 / np.pi).astype(np.float32) * (x + 0.044715 * (x * x * x))))


def _layer_norm(r, g, b):
    mu = jnp.mean(r, axis=-1, keepdims=True)
    d = r - mu
    var = jnp.mean(d * d, axis=-1, keepdims=True)
    return d * lax.rsqrt(var + LN_EPS) * g + b


def _dot(a, b):
    return jnp.dot(a, b, preferred_element_type=F32)


def _split3(x):
    x1 = x.astype(BF16)
    r = x - x1.astype(F32)
    x2 = r.astype(BF16)
    return x1, x2, (r - x2.astype(F32)).astype(BF16)


def _dot_nt(a, b):
    return lax.dot_general(a, b, (((1,), (1,)), ((), ())), preferred_element_type=F32)


def _dot_tn(a, b):
    return lax.dot_general(a, b, (((0,), (0,)), ((), ())), preferred_element_type=F32)


def _lower_tri(n):
    row = lax.broadcasted_iota(jnp.int32, (n, n), 0)
    col = lax.broadcasted_iota(jnp.int32, (n, n), 1)
    return col <= row


def _inproj_kernel(x_ref, w_ref, *o_refs, splits):
    xb = x_ref[...].astype(BF16)
    off = 0
    for o_ref, n in zip(o_refs, splits):
        o_ref[...] = _dot(xb, w_ref[:, off:off + n]).astype(o_ref.dtype)
        off += n


def _inproj(x, w_all, layer, splits, dtypes):
    rows, d = x.shape
    n_total = sum(splits)
    return pl.pallas_call(
        functools.partial(_inproj_kernel, splits=splits),
        grid=(rows // ROW_TILE,),
        in_specs=[pl.BlockSpec((ROW_TILE, d), lambda i: (i, 0)), _layer_spec((d, n_total), layer)],
        out_specs=[pl.BlockSpec((ROW_TILE, n), lambda i: (i, 0)) for n in splits],
        out_shape=[jax.ShapeDtypeStruct((rows, n), dt) for n, dt in zip(splits, dtypes)],
        compiler_params=_params(1),
        name="inproj",
    )(x, w_all)


def _seq_spec(batch, t, n):
    return pl.BlockSpec((batch, t, n), lambda c: (0, c, 0))


def _causal_conv(x_new, xcat_ref, b, cw, cb, width, t):
    halo = SUBLANES
    ys = []
    for s in range(x_new.shape[1] // LANES):
        sl = slice(s * LANES, (s + 1) * LANES)
        xcat_ref[b, s, halo:halo + t, :] = x_new[:, sl]
        y = cb[:, sl] + cw[width - 1:width, sl] * x_new[:, sl]
        for k in range(1, width):
            y = y + cw[width - 1 - k:width - k, sl] * xcat_ref[b, s, halo - k:halo - k + t, :]
        xcat_ref[b, s, 0:halo, :] = xcat_ref[b, s, t:t + halo, :]
        ys.append(y)
    return jnp.concatenate(ys, axis=-1)


def _ssd_kernel(z_ref, xbc_ref, dt_ref, cw_ref, cb_ref, dtb_ref, alog_ref, dskip_ref, nw_ref, expand_ref,
                o_ref, state_ref, xcat_ref):
    t = SSD_CHUNK
    batch = z_ref.shape[0]

    @pl.when(pl.program_id(0) == 0)
    def _():
        state_ref[...] = jnp.zeros_like(state_ref)
        xcat_ref[:, :, 0:SUBLANES, :] = jnp.zeros((batch, SSD_CONV_DIM // LANES, SUBLANES, LANES), F32)

    cw = cw_ref[...]
    cb = cb_ref[...]
    neg_a = -jnp.exp(alog_ref[...])
    expand = expand_ref[...]
    tri = _lower_tri(t)
    tri3 = jnp.concatenate([jnp.where(tri, 1.0, 0.0).astype(BF16)] * 3, axis=1)
    lane = lax.broadcasted_iota(jnp.int32, (t, LANES), 1)
    lo_half = lane < SSD_D_STATE
    sub = lax.broadcasted_iota(jnp.int32, (LANES, LANES), 0)
    heads_per_group = SSD_HEADS // SSD_GROUPS
    gw = SSD_D_INNER // SSD_GROUPS

    for b in range(batch):
        xa = _silu(_causal_conv(xbc_ref[b], xcat_ref, b, cw, cb, SSD_CONV, t))
        xs = xa[:, :SSD_D_INNER]
        bm = xa[:, SSD_D_INNER:SSD_D_INNER + LANES]
        cm = xa[:, SSD_D_INNER + LANES:SSD_D_INNER + 2 * LANES]

        dt = _softplus(dt_ref[b] + dtb_ref[...])
        cs = _dot(tri3, jnp.concatenate(_split3(dt * neg_a), axis=0))
        cs_row = cs.T
        per_channel = _dot(jnp.concatenate(_split3(jnp.concatenate([dt, cs], axis=0)), axis=1), expand)
        dt_f = per_channel[:t, :]
        cs_f = per_channel[t:, :]
        cs_last = cs_f[t - 1:t, :]
        xdt = xs * dt_f
        ecs = jnp.exp(cs_f)
        w_state = (jnp.exp(cs_last - cs_f) * xdt).astype(BF16)
        dec_last = jnp.exp(cs_last)

        bm_t = bm.T.astype(BF16)
        cm_g = [jnp.where(lo_half, cm, 0.0).astype(BF16), jnp.where(lo_half, 0.0, cm).astype(BF16)]
        scores_g = [_dot(c, bm_t) for c in cm_g]

        y_pairs = []
        for pair in range(SSD_HEADS // 2):
            g = (2 * pair) // heads_per_group
            sl = slice(LANES * pair, LANES * (pair + 1))
            x_pair = xdt[:, sl]
            y_pair = None
            for j in range(2):
                h = 2 * pair + j
                x_h = jnp.where(lo_half if j == 0 else jnp.logical_not(lo_half), x_pair, 0.0).astype(BF16)
                diff = cs[:, h:h + 1] - cs_row[h:h + 1, :]
                decay = jnp.where(tri, jnp.exp(jnp.where(tri, diff, 0.0)), 0.0)
                y_h = _dot((scores_g[g] * decay).astype(BF16), x_h)
                y_pair = y_h if y_pair is None else y_pair + y_h
            state = state_ref[b, pair]
            y_pair = y_pair + _dot(cm_g[g], state.astype(BF16)) * ecs[:, sl]
            upd = _dot(bm_t, w_state[:, sl])
            in_group = (sub >= g * SSD_D_STATE) & (sub < (g + 1) * SSD_D_STATE)
            state_ref[b, pair] = jnp.where(in_group, upd, 0.0) + dec_last[:, sl] * state
            y_pairs.append(y_pair)

        y = jnp.concatenate(y_pairs, axis=-1) + dskip_ref[...] * xs
        y = y * _silu(z_ref[b])
        outs = []
        for g in range(SSD_GROUPS):
            yg = y[:, g * gw:(g + 1) * gw]
            ms = jnp.mean(yg * yg, axis=-1, keepdims=True)
            outs.append(yg * lax.rsqrt(ms + RMS_EPS) * nw_ref[:, g * gw:(g + 1) * gw])
        o_ref[b] = jnp.concatenate(outs, axis=-1).astype(o_ref.dtype)


def _ssd_mixer(z, xbc, dt_raw, conv_w, conv_b, dt_bias, a_log, d_skip, norm_w):
    batch, seq, _ = z.shape
    pad_heads = lambda v: jnp.pad(v.astype(F32), (0, LANES - SSD_HEADS)).reshape(1, LANES)
    expand = (jnp.arange(LANES)[:, None] == (jnp.arange(SSD_D_INNER) // SSD_HEAD_DIM)[None, :]).astype(BF16)
    expand = jnp.concatenate([expand] * 3, axis=0)
    return pl.pallas_call(
        _ssd_kernel,
        grid=(seq // SSD_CHUNK,),
        in_specs=[_seq_spec(batch, SSD_CHUNK, SSD_D_INNER), _seq_spec(batch, SSD_CHUNK, SSD_CONV_DIM),
                  _seq_spec(batch, SSD_CHUNK, LANES),
                  _const_spec((SSD_CONV, SSD_CONV_DIM)), _const_spec((1, SSD_CONV_DIM)),
                  _const_spec((1, LANES)), _const_spec((1, LANES)),
                  _const_spec((1, SSD_D_INNER)), _const_spec((1, SSD_D_INNER)),
                  _const_spec((3 * LANES, SSD_D_INNER))],
        out_specs=_seq_spec(batch, SSD_CHUNK, SSD_D_INNER),
        out_shape=jax.ShapeDtypeStruct((batch, seq, SSD_D_INNER), BF16),
        scratch_shapes=[pltpu.VMEM((batch, SSD_HEADS // 2, LANES, LANES), F32),
                        pltpu.VMEM((batch, SSD_CONV_DIM // LANES, SSD_CHUNK + SUBLANES, LANES), F32)],
        compiler_params=_params(1),
        name="ssd_mixer",
    )(z, xbc, dt_raw, conv_w.astype(F32), conv_b.astype(F32).reshape(1, -1), pad_heads(dt_bias), pad_heads(a_log),
      jnp.repeat(d_skip.astype(F32), SSD_HEAD_DIM).reshape(1, -1), norm_w.astype(F32).reshape(1, -1), expand)


def _hgrn2_kernel(hq_ref, hf_ref, hi_ref, hg_ref, lower_ref, nw_ref, o_ref, state_ref, attn_ref, *, layer):
    t = HG_CHUNK
    batch = hq_ref.shape[0]
    rows = batch * t
    stack = lambda ref: jnp.concatenate([ref[b] for b in range(batch)], axis=0)

    @pl.when(pl.program_id(0) == 0)
    def _():
        state_ref[...] = jnp.zeros_like(state_ref)

    low = lower_ref[...]
    ex = jnp.exp(low - jnp.max(low, axis=0, keepdims=True))
    sm = ex / jnp.sum(ex, axis=0, keepdims=True)
    lb = jnp.clip(jnp.sum(sm[:layer + 1, :], axis=0, keepdims=True) - sm[0:1, :], 0.0, 1.0)

    row = lax.broadcasted_iota(jnp.int32, (rows, rows), 0)
    col = lax.broadcasted_iota(jnp.int32, (rows, rows), 1)
    causal = (col <= row) & ((row & -t) == (col & -t))
    causal_b = jnp.where(causal, 1.0, 0.0).astype(BF16)
    head_slices = [slice(h * HG_KEY_DIM, (h + 1) * HG_KEY_DIM) for h in range(HG_HEADS)]

    q = _silu(stack(hq_ref))
    fx = stack(hf_ref)
    e = jnp.exp(-jnp.abs(fx))
    big = 1.0 / (1.0 + e)
    small = e * big
    pos = fx >= 0.0
    log_f = jnp.log(lb + (1.0 - lb) * jnp.where(pos, big, small))
    k = (1.0 - lb) * jnp.where(pos, small, big)
    bc = _dot(jnp.concatenate([causal_b] * 3, axis=1), jnp.concatenate(_split3(log_f), axis=0))
    last_rows = [bc[(b + 1) * t - 1:(b + 1) * t, :] for b in range(batch)]
    b_last = jnp.concatenate([jnp.broadcast_to(r, (t, HG_WIDTH)) for r in last_rows], axis=0)

    mid = 0.5 * b_last
    q_mid = (q * jnp.exp(bc - mid)).astype(BF16)
    k_mid = (k * jnp.exp(mid - bc)).astype(BF16)
    for h, sl in enumerate(head_slices):
        attn_ref[h] = _dot_nt(q_mid[:, sl], k_mid[:, sl])

    @pl.when(jnp.logical_not(jnp.max(-jnp.concatenate(last_rows, axis=0)) <= HG_SAFE_DECAY))
    def _():
        col_id = lax.broadcasted_iota(jnp.int32, (rows, rows), 1)
        row_id = lax.broadcasted_iota(jnp.int32, (rows, HG_KEY_DIM), 0)
        for h, sl in enumerate(head_slices):

            def column(s, acc, q_h=q[:, sl], k_h=k[:, sl], bc_h=bc[:, sl]):
                k_s = jnp.sum(jnp.where(row_id == s, k_h, 0.0), axis=0, keepdims=True)
                bc_s = jnp.sum(jnp.where(row_id == s, bc_h, 0.0), axis=0, keepdims=True)
                w = q_h * k_s * jnp.exp(jnp.minimum(bc_h - bc_s, 0.0))
                return jnp.where(col_id == s, jnp.sum(w, axis=-1, keepdims=True), acc)

            attn_ref[h] = lax.fori_loop(0, rows, column, jnp.zeros((rows, rows), F32))

    vb = stack(hi_ref)
    q_state = (q * jnp.exp(bc)).astype(BF16)
    k_state = (k * jnp.exp(b_last - bc)).astype(BF16)
    outs = []
    for h, sl in enumerate(head_slices):
        o = _dot(jnp.where(causal, attn_ref[h], 0.0).astype(BF16), vb[:, sl])
        o_inter = []
        for b in range(batch):
            rs = slice(b * t, (b + 1) * t)
            state = state_ref[b, h]
            o_inter.append(_dot_nt(q_state[rs, sl], state.astype(BF16)))
            state_ref[b, h] = (state * jnp.exp(last_rows[b][:, sl])
                               + _dot_tn(vb[rs, sl], k_state[rs, sl]))
        o = o + jnp.concatenate(o_inter, axis=0)
        ms = jnp.mean(o * o, axis=-1, keepdims=True)
        outs.append(o * lax.rsqrt(ms + RMS_EPS) * nw_ref[...])
    y = jnp.concatenate(outs, axis=-1) * _silu(stack(hg_ref))
    for b in range(batch):
        o_ref[b] = y[b * t:(b + 1) * t, :].astype(o_ref.dtype)


def _hgrn2_mixer(hq, hf, hi, hg, hg_lower, norm_w, layer):
    batch, seq, _ = hq.shape
    spec = _seq_spec(batch, HG_CHUNK, HG_WIDTH)
    return pl.pallas_call(
        functools.partial(_hgrn2_kernel, layer=layer),
        grid=(seq // HG_CHUNK,),
        in_specs=[spec, spec, spec, spec, _const_spec(hg_lower.shape), _const_spec((1, HG_VAL_DIM))],
        out_specs=spec,
        out_shape=jax.ShapeDtypeStruct((batch, seq, HG_WIDTH), BF16),
        scratch_shapes=[pltpu.VMEM((batch, HG_HEADS, HG_VAL_DIM, HG_KEY_DIM), F32),
                        pltpu.VMEM((HG_HEADS, batch * HG_CHUNK, batch * HG_CHUNK), F32)],
        compiler_params=_params(1),
        name="hgrn2_mixer",
    )(hq, hf, hi, hg, hg_lower.astype(F32), norm_w.astype(F32).reshape(1, -1))


def _swa_kernel(sinks_ref, q_ref, kp_ref, kc_ref, vp_ref, vc_ref, o_ref):
    t = SWA_BLOCK
    batch = q_ref.shape[0]
    group = SWA_Q_HEADS // SWA_KV_HEADS
    rows = group * t

    lane_kv = lax.broadcasted_iota(jnp.int32, (2 * t, LANES), 1)
    lo_kv = lane_kv < SWA_HEAD_DIM
    lane_q = lax.broadcasted_iota(jnp.int32, (t, LANES), 1)
    lo_q = lane_q < SWA_HEAD_DIM
    row = lax.broadcasted_iota(jnp.int32, (rows, 2 * t), 0) & (t - 1)
    col = lax.broadcasted_iota(jnp.int32, (rows, 2 * t), 1)
    first_block = pl.program_id(0) == 0
    visible = (col > row) & (col <= row + t) & ((col >= t) | jnp.logical_not(first_block))
    head_of_row = lax.broadcasted_iota(jnp.int32, (rows, 1), 0) // t

    for b in range(batch):
        q = q_ref[b].astype(F32) * (SWA_HEAD_DIM ** -0.5)
        kcat = jnp.concatenate([kp_ref[b], kc_ref[b]], axis=0).astype(F32)
        vcat = jnp.concatenate([vp_ref[b], vc_ref[b]], axis=0).astype(F32)
        k_swap = pltpu.roll(kcat, SWA_HEAD_DIM, axis=1)
        v_swap = pltpu.roll(vcat, SWA_HEAD_DIM, axis=1)
        outs = []
        for g in range(SWA_KV_HEADS):
            k_dup = (jnp.where(lo_kv, kcat, k_swap) if g == 0 else jnp.where(lo_kv, k_swap, kcat)).astype(BF16)
            v_dup = (jnp.where(lo_kv, vcat, v_swap) if g == 0 else jnp.where(lo_kv, v_swap, vcat)).astype(BF16)
            pieces = []
            sink = jnp.zeros((rows, 1), F32)
            for hh in range(group):
                head = g * group + hh
                q_pair = q[:, (head // 2) * LANES:(head // 2 + 1) * LANES]
                pieces.append(jnp.where(lo_q if head % 2 == 0 else jnp.logical_not(lo_q), q_pair, 0.0))
                sink = jnp.where(head_of_row == hh, sinks_ref[head], sink)
            s = _dot_nt(jnp.concatenate(pieces, axis=0).astype(BF16), k_dup)
            s = jnp.where(visible, s, MASK_VALUE)
            m = jnp.maximum(jnp.max(s, axis=-1, keepdims=True), sink)
            p = jnp.exp(s - m)
            denom = jnp.sum(p, axis=-1, keepdims=True) + jnp.exp(sink - m)
            o = _dot((p / denom).astype(BF16), v_dup)
            for pp in range(group // 2):
                outs.append(jnp.where(lo_q, o[2 * pp * t:(2 * pp + 1) * t, :], o[(2 * pp + 1) * t:(2 * pp + 2) * t, :]))
        o_ref[b] = jnp.concatenate(outs, axis=-1).astype(o_ref.dtype)


def _swa_attention(q, k, v, sinks):
    batch, seq, _ = q.shape
    cur = lambda n: pl.BlockSpec((batch, SWA_BLOCK, n), lambda i: (0, i, 0))
    prev = lambda n: pl.BlockSpec((batch, SWA_BLOCK, n), lambda i: (0, jnp.maximum(i - 1, 0), 0))
    return pl.pallas_call(
        _swa_kernel,
        grid=(seq // SWA_BLOCK,),
        in_specs=[pl.BlockSpec(memory_space=pltpu.SMEM), cur(SWA_Q_DIM),
                  prev(SWA_KV_DIM), cur(SWA_KV_DIM), prev(SWA_KV_DIM), cur(SWA_KV_DIM)],
        out_specs=cur(SWA_Q_DIM),
        out_shape=jax.ShapeDtypeStruct((batch, seq, SWA_Q_DIM), BF16),
        compiler_params=_params(1),
        name="swa_attention",
    )(sinks.astype(F32), q, k, k, v, v)


def _rglru_kernel(gate_ref, xr_ref, cw_ref, cb_ref, wg_ref, bg_ref, lam_ref, o_ref, h_ref, xcat_ref):
    t = RG_CHUNK
    batch = gate_ref.shape[0]
    groups = t // SUBLANES

    @pl.when(pl.program_id(0) == 0)
    def _():
        h_ref[...] = jnp.zeros_like(h_ref)
        xcat_ref[:, :, 0:SUBLANES, :] = jnp.zeros((batch, RG_WIDTH // LANES, SUBLANES, LANES), F32)

    cw = cw_ref[...]
    cb = cb_ref[...]
    decay_rate = (-RG_C) * _softplus(-lam_ref[...])
    sub = lax.broadcasted_iota(jnp.int32, (groups, SUBLANES, LANES), 1)
    for b in range(batch):
        xc = _causal_conv(xr_ref[b], xcat_ref, b, cw, cb, RG_CONV, t)
        gates = _dot(xc.astype(BF16), wg_ref[...]) + bg_ref[...]
        log_a = decay_rate * _sigmoid(gates[:, :RG_WIDTH])
        a = jnp.exp(log_a)
        th = jnp.tanh(log_a)
        u = jnp.sqrt(jnp.maximum(-2.0 * th / (1.0 - th), 0.0)) * (_sigmoid(gates[:, RG_WIDTH:]) * xc)
        outs = []
        for c in range(RG_WIDTH // LANES):
            sl = slice(c * LANES, (c + 1) * LANES)
            a_c = a[:, sl].reshape(groups, SUBLANES, LANES)
            u_c = u[:, sl].reshape(groups, SUBLANES, LANES)
            d = 1
            while d < SUBLANES:
                keep = sub >= d
                u_c = u_c + a_c * jnp.where(keep, pltpu.roll(u_c, d, axis=1), 0.0)
                a_c = a_c * jnp.where(keep, pltpu.roll(a_c, d, axis=1), 1.0)
                d *= 2
            h_prev = h_ref[b, 0:1, sl]
            h_groups = []
            for g in range(groups):
                h_g = u_c[g] + a_c[g] * h_prev
                h_prev = h_g[SUBLANES - 1:SUBLANES, :]
                h_groups.append(h_g)
            h_ref[b, 0:1, sl] = h_prev
            outs.append(jnp.concatenate(h_groups, axis=0))
        o_ref[b] = (jnp.concatenate(outs, axis=-1) * _gelu_tanh(gate_ref[b])).astype(o_ref.dtype)


def _block_diag(w):
    nb, d, _ = w.shape
    eye = jnp.eye(nb, dtype=w.dtype)
    return (eye[:, None, :, None] * w[:, :, None, :]).reshape(nb * d, nb * d)


def _rglru_mixer(gate, xr, conv_w, conv_b, wa, ba, wx, bx, lam):
    batch, seq, _ = gate.shape
    spec = _seq_spec(batch, RG_CHUNK, RG_WIDTH)
    w_gates = jnp.concatenate([_block_diag(wa.astype(F32)), _block_diag(wx.astype(F32))], axis=1).astype(BF16)
    b_gates = jnp.concatenate([ba, bx]).astype(F32).reshape(1, -1)
    return pl.pallas_call(
        _rglru_kernel,
        grid=(seq // RG_CHUNK,),
        in_specs=[spec, spec, _const_spec((RG_CONV, RG_WIDTH)), _const_spec((1, RG_WIDTH)),
                  _const_spec((RG_WIDTH, 2 * RG_WIDTH)), _const_spec((1, 2 * RG_WIDTH)),
                  _const_spec((1, RG_WIDTH))],
        out_specs=spec,
        out_shape=jax.ShapeDtypeStruct((batch, seq, RG_WIDTH), BF16),
        scratch_shapes=[pltpu.VMEM((batch, SUBLANES, RG_WIDTH), F32),
                        pltpu.VMEM((batch, RG_WIDTH // LANES, RG_CHUNK + SUBLANES, LANES), F32)],
        compiler_params=_params(1),
        name="rglru_mixer",
    )(gate, xr, conv_w.astype(F32), conv_b.astype(F32).reshape(1, -1), w_gates, b_gates,
      lam.astype(F32).reshape(1, -1))


def _ffn_up(xb, wup_ref, cw_ref, cb_ref, carry_ref, h_ref, act_ref):
    tm = ROW_TILE
    halo = SUBLANES
    fc = FFN_COL_CHUNK
    for c in range(FFN_DIM // fc):
        h = [_dot(xb, wup_ref[:, part * FFN_DIM + c * fc:part * FFN_DIM + (c + 1) * fc]) for part in range(2)]
        for s in range(fc // LANES):
            conv = []
            for part in range(2):
                col = part * FFN_DIM + c * fc + s * LANES
                hb = h_ref.at[c % 2, part, s]
                hb[0:halo, :] = carry_ref[:, col:col + LANES]
                hb[halo:halo + tm, :] = h[part][:, s * LANES:(s + 1) * LANES]
                carry_ref[:, col:col + LANES] = hb[tm:tm + halo, :]
                y = cb_ref[:, col:col + LANES] + cw_ref[2:3, col:col + LANES] * hb[halo:halo + tm, :]
                for k in range(1, FFN_CONV):
                    y = y + cw_ref[2 - k:3 - k, col:col + LANES] * hb[halo - k:halo - k + tm, :]
                conv.append(y)
            act_col = c * fc + s * LANES
            act_ref[:, act_col:act_col + LANES] = (_silu(conv[0]) * conv[1]).astype(BF16)


def _tail_kernel(ya_ref, yb_ref, x_ref, wout_ref, lng_ref, lnb_ref, wup_ref, cw_ref, cb_ref, wdown_ref,
                 o_ref, carry_ref, h_ref, act_ref, x1_ref):
    @pl.when(pl.program_id(1) == 0)
    def _():
        carry_ref[...] = jnp.zeros_like(carry_ref)

    half = ya_ref.shape[1]
    m = _dot(ya_ref[...], wout_ref[0:half, :]) + _dot(yb_ref[...], wout_ref[half:2 * half, :])
    x1_ref[...] = _layer_norm(ALPHA * x_ref[...] + m, lng_ref[0:1, :], lnb_ref[0:1, :])
    _ffn_up(x1_ref[...].astype(BF16), wup_ref, cw_ref, cb_ref, carry_ref, h_ref, act_ref)
    f = _dot(act_ref[...], wdown_ref[...])
    o_ref[...] = _layer_norm(ALPHA * x1_ref[...] + f, lng_ref[1:2, :], lnb_ref[1:2, :])


def _layer_tail(ya, yb, x, w_out_all, j, ln_g_all, ln_b_all, w_up_all, conv_w_all, conv_b_all, w_down_all, layer, batch):
    rows, d = x.shape
    half = ya.shape[1]
    ntile = rows // batch // ROW_TILE
    row_spec = lambda n: pl.BlockSpec((ROW_TILE, n), lambda b, t: (b * ntile + t, 0))
    return pl.pallas_call(
        _tail_kernel,
        grid=(batch, ntile),
        in_specs=[row_spec(half), row_spec(half), row_spec(d), _layer_spec((2 * half, d), j),
                  _layer_spec((2, d), layer), _layer_spec((2, d), layer),
                  _layer_spec((d, 2 * FFN_DIM), layer), _layer_spec((FFN_CONV, 2 * FFN_DIM), layer),
                  _layer_spec((1, 2 * FFN_DIM), layer), _layer_spec((FFN_DIM, d), layer)],
        out_specs=row_spec(d),
        out_shape=jax.ShapeDtypeStruct((rows, d), F32),
        scratch_shapes=[pltpu.VMEM((SUBLANES, 2 * FFN_DIM), F32),
                        pltpu.VMEM((2, 2, FFN_COL_CHUNK // LANES, ROW_TILE + SUBLANES, LANES), F32),
                        pltpu.VMEM((ROW_TILE, FFN_DIM), BF16),
                        pltpu.VMEM((ROW_TILE, d), F32)],
        compiler_params=_params(2),
        name="layer_tail",
    )(ya, yb, x, w_out_all, ln_g_all, ln_b_all, w_up_all, conv_w_all, conv_b_all, w_down_all)


AB_SPLITS = (SSD_D_INNER, SSD_CONV_DIM, LANES, HG_WIDTH, HG_WIDTH, HG_WIDTH, HG_WIDTH)
CD_SPLITS = (SWA_Q_DIM, SWA_KV_DIM, SWA_KV_DIM, RG_WIDTH, RG_WIDTH)
AB_DTYPES = (F32, F32, F32, F32, F32, BF16, F32)
CD_DTYPES = (BF16, BF16, BF16, F32, F32)


def _ab_weight(w_in):
    a = SSD_D_INNER + SSD_CONV_DIM
    dt_cols = jnp.pad(w_in[..., a:a + SSD_HEADS], ((0, 0), (0, 0), (0, LANES - SSD_HEADS)))
    return jnp.concatenate([w_in[..., :a], dt_cols, w_in[..., a + SSD_HEADS:]], axis=-1).astype(BF16)


def kernel(x, ab_w_in, ssd_conv_w, ssd_conv_b, ssd_dt_bias, ssd_a_log, ssd_d, ssd_norm_w, hg_lower, hg_norm_w, ab_w_out, cd_w_in, swa_sinks, rg_conv_w, rg_conv_b, rg_wa, rg_ba, rg_wx, rg_bx, rg_lambda, cd_w_out, ffn_w_up, ffn_conv_w, ffn_conv_b, ffn_w_down, ln_g, ln_b):
    batch, seq, d = x.shape
    assert d == D_MODEL and seq % ROW_TILE == 0
    rows = batch * seq
    seq_view = lambda arrs: [a.reshape(batch, seq, a.shape[-1]) for a in arrs]
    xr = x.reshape(rows, d).astype(F32)
    ab_w_in_b, cd_w_in_b = _ab_weight(ab_w_in), cd_w_in.astype(BF16)
    ab_w_out_b, cd_w_out_b = ab_w_out.astype(BF16), cd_w_out.astype(BF16)
    w_up_b, w_down_b = ffn_w_up.astype(BF16), ffn_w_down.astype(BF16)
    conv_w, conv_b = ffn_conv_w.astype(F32), ffn_conv_b.astype(F32).reshape(DEPTH, 1, 2 * FFN_DIM)
    ln_g, ln_b = ln_g.astype(F32), ln_b.astype(F32)
    for layer in range(DEPTH):
        j = layer // 2
        if layer % 2 == 0:
            z, xbc, dt_raw, hq, hf, hi, hg = seq_view(_inproj(xr, ab_w_in_b, j, AB_SPLITS, AB_DTYPES))
            ya = _ssd_mixer(z, xbc, dt_raw, ssd_conv_w[j], ssd_conv_b[j], ssd_dt_bias[j], ssd_a_log[j], ssd_d[j],
                            ssd_norm_w[j])
            yb = _hgrn2_mixer(hq, hf, hi, hg, hg_lower, hg_norm_w[j], j)
            w_out_b = ab_w_out_b
        else:
            q, k, v, gate, xg = seq_view(_inproj(xr, cd_w_in_b, j, CD_SPLITS, CD_DTYPES))
            ya = _swa_attention(q, k, v, swa_sinks[j])
            yb = _rglru_mixer(gate, xg, rg_conv_w[j], rg_conv_b[j], rg_wa[j], rg_ba[j], rg_wx[j], rg_bx[j],
                              rg_lambda[j])
            w_out_b = cd_w_out_b
        xr = _layer_tail(ya.reshape(rows, -1), yb.reshape(rows, -1), xr, w_out_b, j, ln_g, ln_b,
                         w_up_b, conv_w, conv_b, w_down_b, layer, batch)
    return xr.reshape(batch, seq, d).astype(x.dtype)
```

```python
import functools
import itertools

import jax
import jax.numpy as jnp
import numpy as np
from jax import lax
from jax.experimental import pallas as pl
from jax.experimental.pallas import tpu as pltpu

F32 = jnp.float32
BF16 = jnp.bfloat16

LANES = 128
SUBLANES = 8
VMEM_LIMIT_BYTES = 56 * 1024 * 1024

D_MODEL = 1024
DEPTH = 4
SSD_HEADS = 8
SSD_HEAD_DIM = 64
SSD_D_INNER = SSD_HEADS * SSD_HEAD_DIM
SSD_GROUPS = 2
SSD_D_STATE = 64
SSD_CONV = 4
SSD_CONV_DIM = SSD_D_INNER + 2 * SSD_GROUPS * SSD_D_STATE
SSD_CHUNK = 128
HG_HEADS = 4
HG_KEY_DIM = 128
HG_VAL_DIM = 128
HG_WIDTH = HG_HEADS * HG_KEY_DIM
HG_CHUNK = 64
HG_SAFE_DECAY = 160.0
SWA_Q_HEADS = 8
SWA_KV_HEADS = 2
SWA_HEAD_DIM = 64
SWA_BLOCK = 128
SWA_Q_DIM = SWA_Q_HEADS * SWA_HEAD_DIM
SWA_KV_DIM = SWA_KV_HEADS * SWA_HEAD_DIM
RG_WIDTH = 512
RG_BLOCKS = 8
RG_BLOCK_DIM = RG_WIDTH // RG_BLOCKS
RG_CONV = 4
RG_C = 8.0
RG_CHUNK = 128
FFN_DIM = 2816
FFN_CONV = 3
FFN_COL_CHUNK = 256
ROW_TILE = 512
CHUNKS_PER_STEP = 2
LN_EPS = 1e-5
RMS_EPS = 1e-6
MASK_VALUE = -1e9
ALPHA = (2 * DEPTH) ** 0.25


def _params(n_axes):
    return pltpu.CompilerParams(dimension_semantics=("arbitrary",) * n_axes,
                                vmem_limit_bytes=VMEM_LIMIT_BYTES)


def _const_spec(shape):
    return pl.BlockSpec(shape, lambda *_: (0,) * len(shape), pipeline_mode=pl.Buffered(1))


def _layer_spec(shape, layer):
    return pl.BlockSpec((None,) + tuple(shape), lambda *_: (layer,) + (0,) * len(shape),
                        pipeline_mode=pl.Buffered(1))


def _sigmoid(x):
    return 1.0 / (1.0 + jnp.exp(-x))


def _silu(x):
    return x * _sigmoid(x)


def _softplus(x):
    return jnp.maximum(x, 0.0) + jnp.log1p(jnp.exp(-jnp.abs(x)))


def _gelu_tanh(x):
    return 0.5 * x * (1.0 + jnp.tanh(np.sqrt(2.0 / np.pi).astype(np.float32) * (x + 0.044715 * (x * x * x))))


def _layer_norm(r, g, b):
    mu = jnp.mean(r, axis=-1, keepdims=True)
    d = r - mu
    var = jnp.mean(d * d, axis=-1, keepdims=True)
    return d * lax.rsqrt(var + LN_EPS) * g + b


def _dot(a, b):
    return jnp.dot(a, b, preferred_element_type=F32)


def _split3(x):
    x1 = x.astype(BF16)
    r = x - x1.astype(F32)
    x2 = r.astype(BF16)
    return x1, x2, (r - x2.astype(F32)).astype(BF16)


def _dot_nt(a, b):
    return lax.dot_general(a, b, (((1,), (1,)), ((), ())), preferred_element_type=F32)


def _dot_tn(a, b):
    return lax.dot_general(a, b, (((0,), (0,)), ((), ())), preferred_element_type=F32)


def _lower_tri(n):
    row = lax.broadcasted_iota(jnp.int32, (n, n), 0)
    col = lax.broadcasted_iota(jnp.int32, (n, n), 1)
    return col <= row


def _inproj_kernel(x_ref, w_ref, *o_refs, splits):
    xb = x_ref[...].astype(BF16)
    off = 0
    for o_ref, n in zip(o_refs, splits):
        o_ref[...] = _dot(xb, w_ref[:, off:off + n]).astype(o_ref.dtype)
        off += n


def _inproj(x, w_all, layer, splits, dtypes):
    rows, d = x.shape
    n_total = sum(splits)
    return pl.pallas_call(
        functools.partial(_inproj_kernel, splits=splits),
        grid=(rows // ROW_TILE,),
        in_specs=[pl.BlockSpec((ROW_TILE, d), lambda i: (i, 0)), _layer_spec((d, n_total), layer)],
        out_specs=[pl.BlockSpec((ROW_TILE, n), lambda i: (i, 0)) for n in splits],
        out_shape=[jax.ShapeDtypeStruct((rows, n), dt) for n, dt in zip(splits, dtypes)],
        compiler_params=_params(1),
        name="inproj",
    )(x, w_all)


def _seq_spec(batch, t, n):
    return pl.BlockSpec((batch, CHUNKS_PER_STEP * t, n), lambda c: (0, c, 0))


def _chunks_of_step(t, batch):
    return [(c, slice(c * t, (c + 1) * t), b) for c, b in itertools.product(range(CHUNKS_PER_STEP), range(batch))]


def _causal_conv(x_new, xcat_ref, b, cw, cb, width, t):
    halo = SUBLANES
    ys = []
    for s in range(x_new.shape[1] // LANES):
        sl = slice(s * LANES, (s + 1) * LANES)
        xcat_ref[b, s, halo:halo + t, :] = x_new[:, sl]
        y = cb[:, sl] + cw[width - 1:width, sl] * x_new[:, sl]
        for k in range(1, width):
            y = y + cw[width - 1 - k:width - k, sl] * xcat_ref[b, s, halo - k:halo - k + t, :]
        xcat_ref[b, s, 0:halo, :] = xcat_ref[b, s, t:t + halo, :]
        ys.append(y)
    return jnp.concatenate(ys, axis=-1)


def _ssd_kernel(z_ref, xbc_ref, dt_ref, cw_ref, cb_ref, dtb_ref, alog_ref, dskip_ref, nw_ref, expand_ref,
                o_ref, state_ref, xcat_ref):
    t = SSD_CHUNK
    batch = z_ref.shape[0]

    @pl.when(pl.program_id(0) == 0)
    def _():
        state_ref[...] = jnp.zeros_like(state_ref)
        xcat_ref[:, :, 0:SUBLANES, :] = jnp.zeros((batch, SSD_CONV_DIM // LANES, SUBLANES, LANES), F32)

    cw = cw_ref[...]
    cb = cb_ref[...]
    neg_a = -jnp.exp(alog_ref[...])
    expand = expand_ref[...]
    tri = _lower_tri(t)
    tri3 = jnp.concatenate([jnp.where(tri, 1.0, 0.0).astype(BF16)] * 3, axis=1)
    lane = lax.broadcasted_iota(jnp.int32, (t, LANES), 1)
    lo_half = lane < SSD_D_STATE
    sub = lax.broadcasted_iota(jnp.int32, (LANES, LANES), 0)
    heads_per_group = SSD_HEADS // SSD_GROUPS
    gw = SSD_D_INNER // SSD_GROUPS

    for _, rows, b in _chunks_of_step(t, batch):
        xa = _silu(_causal_conv(xbc_ref[b, rows, :], xcat_ref, b, cw, cb, SSD_CONV, t))
        xs = xa[:, :SSD_D_INNER]
        bm = xa[:, SSD_D_INNER:SSD_D_INNER + LANES]
        cm = xa[:, SSD_D_INNER + LANES:SSD_D_INNER + 2 * LANES]

        dt = _softplus(dt_ref[b, rows, :] + dtb_ref[...])
        cs = _dot(tri3, jnp.concatenate(_split3(dt * neg_a), axis=0))
        cs_row = cs.T
        per_channel = _dot(jnp.concatenate(_split3(jnp.concatenate([dt, cs], axis=0)), axis=1), expand)
        dt_f = per_channel[:t, :]
        cs_f = per_channel[t:, :]
        cs_last = cs_f[t - 1:t, :]
        xdt = xs * dt_f
        ecs = jnp.exp(cs_f)
        w_state = (jnp.exp(cs_last - cs_f) * xdt).astype(BF16)
        dec_last = jnp.exp(cs_last)

        bm_t = bm.T.astype(BF16)
        cm_g = [jnp.where(lo_half, cm, 0.0).astype(BF16), jnp.where(lo_half, 0.0, cm).astype(BF16)]
        scores_g = [_dot(c, bm_t) for c in cm_g]

        y_pairs = []
        for pair in range(SSD_HEADS // 2):
            g = (2 * pair) // heads_per_group
            sl = slice(LANES * pair, LANES * (pair + 1))
            x_pair = xdt[:, sl]
            y_pair = None
            for j in range(2):
                h = 2 * pair + j
                x_h = jnp.where(lo_half if j == 0 else jnp.logical_not(lo_half), x_pair, 0.0).astype(BF16)
                diff = cs[:, h:h + 1] - cs_row[h:h + 1, :]
                decay = jnp.where(tri, jnp.exp(jnp.where(tri, diff, 0.0)), 0.0)
                y_h = _dot((scores_g[g] * decay).astype(BF16), x_h)
                y_pair = y_h if y_pair is None else y_pair + y_h
            state = state_ref[b, pair]
            y_pair = y_pair + _dot(cm_g[g], state.astype(BF16)) * ecs[:, sl]
            upd = _dot(bm_t, w_state[:, sl])
            in_group = (sub >= g * SSD_D_STATE) & (sub < (g + 1) * SSD_D_STATE)
            state_ref[b, pair] = jnp.where(in_group, upd, 0.0) + dec_last[:, sl] * state
            y_pairs.append(y_pair)

        y = jnp.concatenate(y_pairs, axis=-1) + dskip_ref[...] * xs
        y = y * _silu(z_ref[b, rows, :])
        outs = []
        for g in range(SSD_GROUPS):
            yg = y[:, g * gw:(g + 1) * gw]
            ms = jnp.mean(yg * yg, axis=-1, keepdims=True)
            outs.append(yg * lax.rsqrt(ms + RMS_EPS) * nw_ref[:, g * gw:(g + 1) * gw])
        o_ref[b, rows, :] = jnp.concatenate(outs, axis=-1).astype(o_ref.dtype)


def _ssd_mixer(z, xbc, dt_raw, conv_w, conv_b, dt_bias, a_log, d_skip, norm_w):
    batch, seq, _ = z.shape
    pad_heads = lambda v: jnp.pad(v.astype(F32), (0, LANES - SSD_HEADS)).reshape(1, LANES)
    expand = (jnp.arange(LANES)[:, None] == (jnp.arange(SSD_D_INNER) // SSD_HEAD_DIM)[None, :]).astype(BF16)
    expand = jnp.concatenate([expand] * 3, axis=0)
    return pl.pallas_call(
        _ssd_kernel,
        grid=(seq // (CHUNKS_PER_STEP * SSD_CHUNK),),
        in_specs=[_seq_spec(batch, SSD_CHUNK, SSD_D_INNER), _seq_spec(batch, SSD_CHUNK, SSD_CONV_DIM),
                  _seq_spec(batch, SSD_CHUNK, LANES),
                  _const_spec((SSD_CONV, SSD_CONV_DIM)), _const_spec((1, SSD_CONV_DIM)),
                  _const_spec((1, LANES)), _const_spec((1, LANES)),
                  _const_spec((1, SSD_D_INNER)), _const_spec((1, SSD_D_INNER)),
                  _const_spec((3 * LANES, SSD_D_INNER))],
        out_specs=_seq_spec(batch, SSD_CHUNK, SSD_D_INNER),
        out_shape=jax.ShapeDtypeStruct((batch, seq, SSD_D_INNER), BF16),
        scratch_shapes=[pltpu.VMEM((batch, SSD_HEADS // 2, LANES, LANES), F32),
                        pltpu.VMEM((batch, SSD_CONV_DIM // LANES, SSD_CHUNK + SUBLANES, LANES), F32)],
        compiler_params=_params(1),
        name="ssd_mixer",
    )(z, xbc, dt_raw, conv_w.astype(F32), conv_b.astype(F32).reshape(1, -1), pad_heads(dt_bias), pad_heads(a_log),
      jnp.repeat(d_skip.astype(F32), SSD_HEAD_DIM).reshape(1, -1), norm_w.astype(F32).reshape(1, -1), expand)


def _hgrn2_kernel(hq_ref, hf_ref, hi_ref, hg_ref, lower_ref, nw_ref, o_ref, state_ref, attn_ref, *, layer):
    t = HG_CHUNK
    batch = hq_ref.shape[0]
    rows = batch * t

    @pl.when(pl.program_id(0) == 0)
    def _():
        state_ref[...] = jnp.zeros_like(state_ref)

    low = lower_ref[...]
    ex = jnp.exp(low - jnp.max(low, axis=0, keepdims=True))
    sm = ex / jnp.sum(ex, axis=0, keepdims=True)
    lb = jnp.clip(jnp.sum(sm[:layer + 1, :], axis=0, keepdims=True) - sm[0:1, :], 0.0, 1.0)

    row = lax.broadcasted_iota(jnp.int32, (rows, rows), 0)
    col = lax.broadcasted_iota(jnp.int32, (rows, rows), 1)
    causal = (col <= row) & ((row & -t) == (col & -t))
    causal_b = jnp.where(causal, 1.0, 0.0).astype(BF16)
    head_slices = [slice(h * HG_KEY_DIM, (h + 1) * HG_KEY_DIM) for h in range(HG_HEADS)]

    for c in range(CHUNKS_PER_STEP):
        stack = lambda ref, c=c: jnp.concatenate([ref[b, c * t:(c + 1) * t, :] for b in range(batch)], axis=0)
        q = _silu(stack(hq_ref))
        fx = stack(hf_ref)
        e = jnp.exp(-jnp.abs(fx))
        big = 1.0 / (1.0 + e)
        small = e * big
        pos = fx >= 0.0
        log_f = jnp.log(lb + (1.0 - lb) * jnp.where(pos, big, small))
        k = (1.0 - lb) * jnp.where(pos, small, big)
        bc = _dot(jnp.concatenate([causal_b] * 3, axis=1), jnp.concatenate(_split3(log_f), axis=0))
        last_rows = [bc[(b + 1) * t - 1:(b + 1) * t, :] for b in range(batch)]
        b_last = jnp.concatenate([jnp.broadcast_to(r, (t, HG_WIDTH)) for r in last_rows], axis=0)

        mid = 0.5 * b_last
        q_mid = (q * jnp.exp(bc - mid)).astype(BF16)
        k_mid = (k * jnp.exp(mid - bc)).astype(BF16)
        for h, sl in enumerate(head_slices):
            attn_ref[h] = _dot_nt(q_mid[:, sl], k_mid[:, sl])

        @pl.when(jnp.logical_not(jnp.max(-jnp.concatenate(last_rows, axis=0)) <= HG_SAFE_DECAY))
        def _():
            col_id = lax.broadcasted_iota(jnp.int32, (rows, rows), 1)
            row_id = lax.broadcasted_iota(jnp.int32, (rows, HG_KEY_DIM), 0)
            for h, sl in enumerate(head_slices):

                def column(s, acc, q_h=q[:, sl], k_h=k[:, sl], bc_h=bc[:, sl]):
                    k_s = jnp.sum(jnp.where(row_id == s, k_h, 0.0), axis=0, keepdims=True)
                    bc_s = jnp.sum(jnp.where(row_id == s, bc_h, 0.0), axis=0, keepdims=True)
                    w = q_h * k_s * jnp.exp(jnp.minimum(bc_h - bc_s, 0.0))
                    return jnp.where(col_id == s, jnp.sum(w, axis=-1, keepdims=True), acc)

                attn_ref[h] = lax.fori_loop(0, rows, column, jnp.zeros((rows, rows), F32))

        vb = stack(hi_ref)
        q_state = (q * jnp.exp(bc)).astype(BF16)
        k_state = (k * jnp.exp(b_last - bc)).astype(BF16)
        outs = []
        for h, sl in enumerate(head_slices):
            o = _dot(jnp.where(causal, attn_ref[h], 0.0).astype(BF16), vb[:, sl])
            o_inter = []
            for b in range(batch):
                rs = slice(b * t, (b + 1) * t)
                state = state_ref[b, h]
                o_inter.append(_dot_nt(q_state[rs, sl], state.astype(BF16)))
                state_ref[b, h] = (state * jnp.exp(last_rows[b][:, sl])
                                   + _dot_tn(vb[rs, sl], k_state[rs, sl]))
            o = o + jnp.concatenate(o_inter, axis=0)
            ms = jnp.mean(o * o, axis=-1, keepdims=True)
            outs.append(o * lax.rsqrt(ms + RMS_EPS) * nw_ref[...])
        y = jnp.concatenate(outs, axis=-1) * _silu(stack(hg_ref))
        for b in range(batch):
            o_ref[b, c * t:(c + 1) * t, :] = y[b * t:(b + 1) * t, :].astype(o_ref.dtype)


def _hgrn2_mixer(hq, hf, hi, hg, hg_lower, norm_w, layer):
    batch, seq, _ = hq.shape
    spec = _seq_spec(batch, HG_CHUNK, HG_WIDTH)
    return pl.pallas_call(
        functools.partial(_hgrn2_kernel, layer=layer),
        grid=(seq // (CHUNKS_PER_STEP * HG_CHUNK),),
        in_specs=[spec, spec, spec, spec, _const_spec(hg_lower.shape), _const_spec((1, HG_VAL_DIM))],
        out_specs=spec,
        out_shape=jax.ShapeDtypeStruct((batch, seq, HG_WIDTH), BF16),
        scratch_shapes=[pltpu.VMEM((batch, HG_HEADS, HG_VAL_DIM, HG_KEY_DIM), F32),
                        pltpu.VMEM((HG_HEADS, batch * HG_CHUNK, batch * HG_CHUNK), F32)],
        compiler_params=_params(1),
        name="hgrn2_mixer",
    )(hq, hf, hi, hg, hg_lower.astype(F32), norm_w.astype(F32).reshape(1, -1))


def _swa_kernel(sinks_ref, q_ref, kp_ref, kc_ref, vp_ref, vc_ref, o_ref):
    t = SWA_BLOCK
    batch = q_ref.shape[0]
    group = SWA_Q_HEADS // SWA_KV_HEADS
    rows = group * t

    lane_kv = lax.broadcasted_iota(jnp.int32, (2 * t, LANES), 1)
    lo_kv = lane_kv < SWA_HEAD_DIM
    lane_q = lax.broadcasted_iota(jnp.int32, (t, LANES), 1)
    lo_q = lane_q < SWA_HEAD_DIM
    row = lax.broadcasted_iota(jnp.int32, (rows, 2 * t), 0) & (t - 1)
    col = lax.broadcasted_iota(jnp.int32, (rows, 2 * t), 1)
    band = (col > row) & (col <= row + t)
    head_of_row = lax.broadcasted_iota(jnp.int32, (rows, 1), 0) // t

    for c, qrows, b in _chunks_of_step(t, batch):
        q = q_ref[b, qrows, :].astype(F32) * (SWA_HEAD_DIM ** -0.5)
        if c == 0:
            visible = band & ((col >= t) | (pl.program_id(0) > 0))
            kcat = jnp.concatenate([kp_ref[b], kc_ref[b, 0:t, :]], axis=0).astype(F32)
            vcat = jnp.concatenate([vp_ref[b], vc_ref[b, 0:t, :]], axis=0).astype(F32)
        else:
            visible = band
            kcat = kc_ref[b, (c - 1) * t:(c + 1) * t, :].astype(F32)
            vcat = vc_ref[b, (c - 1) * t:(c + 1) * t, :].astype(F32)
        k_swap = pltpu.roll(kcat, SWA_HEAD_DIM, axis=1)
        v_swap = pltpu.roll(vcat, SWA_HEAD_DIM, axis=1)
        outs = []
        for g in range(SWA_KV_HEADS):
            k_dup = (jnp.where(lo_kv, kcat, k_swap) if g == 0 else jnp.where(lo_kv, k_swap, kcat)).astype(BF16)
            v_dup = (jnp.where(lo_kv, vcat, v_swap) if g == 0 else jnp.where(lo_kv, v_swap, vcat)).astype(BF16)
            pieces = []
            sink = jnp.zeros((rows, 1), F32)
            for hh in range(group):
                head = g * group + hh
                q_pair = q[:, (head // 2) * LANES:(head // 2 + 1) * LANES]
                pieces.append(jnp.where(lo_q if head % 2 == 0 else jnp.logical_not(lo_q), q_pair, 0.0))
                sink = jnp.where(head_of_row == hh, sinks_ref[head], sink)
            s = _dot_nt(jnp.concatenate(pieces, axis=0).astype(BF16), k_dup)
            s = jnp.where(visible, s, MASK_VALUE)
            m = jnp.maximum(jnp.max(s, axis=-1, keepdims=True), sink)
            p = jnp.exp(s - m)
            denom = jnp.sum(p, axis=-1, keepdims=True) + jnp.exp(sink - m)
            o = _dot((p / denom).astype(BF16), v_dup)
            for pp in range(group // 2):
                outs.append(jnp.where(lo_q, o[2 * pp * t:(2 * pp + 1) * t, :], o[(2 * pp + 1) * t:(2 * pp + 2) * t, :]))
        o_ref[b, qrows, :] = jnp.concatenate(outs, axis=-1).astype(o_ref.dtype)


def _swa_attention(q, k, v, sinks):
    batch, seq, _ = q.shape
    cur = lambda n: pl.BlockSpec((batch, CHUNKS_PER_STEP * SWA_BLOCK, n), lambda i: (0, i, 0))
    prev = lambda n: pl.BlockSpec((batch, SWA_BLOCK, n), lambda i: (0, jnp.maximum(CHUNKS_PER_STEP * i - 1, 0), 0))
    return pl.pallas_call(
        _swa_kernel,
        grid=(seq // (CHUNKS_PER_STEP * SWA_BLOCK),),
        in_specs=[pl.BlockSpec(memory_space=pltpu.SMEM), cur(SWA_Q_DIM),
                  prev(SWA_KV_DIM), cur(SWA_KV_DIM), prev(SWA_KV_DIM), cur(SWA_KV_DIM)],
        out_specs=cur(SWA_Q_DIM),
        out_shape=jax.ShapeDtypeStruct((batch, seq, SWA_Q_DIM), BF16),
        compiler_params=_params(1),
        name="swa_attention",
    )(sinks.astype(F32), q, k, k, v, v)


def _rglru_kernel(gate_ref, xr_ref, cw_ref, cb_ref, wg_ref, bg_ref, lam_ref, o_ref, h_ref, xcat_ref):
    t = RG_CHUNK
    batch = gate_ref.shape[0]
    groups = t // SUBLANES

    @pl.when(pl.program_id(0) == 0)
    def _():
        h_ref[...] = jnp.zeros_like(h_ref)
        xcat_ref[:, :, 0:SUBLANES, :] = jnp.zeros((batch, RG_WIDTH // LANES, SUBLANES, LANES), F32)

    cw = cw_ref[...]
    cb = cb_ref[...]
    decay_rate = (-RG_C) * _softplus(-lam_ref[...])
    sub = lax.broadcasted_iota(jnp.int32, (groups, SUBLANES, LANES), 1)
    for _, rows, b in _chunks_of_step(t, batch):
        xc = _causal_conv(xr_ref[b, rows, :], xcat_ref, b, cw, cb, RG_CONV, t)
        gates = _dot(xc.astype(BF16), wg_ref[...]) + bg_ref[...]
        log_a = decay_rate * _sigmoid(gates[:, :RG_WIDTH])
        a = jnp.exp(log_a)
        th = jnp.tanh(log_a)
        u = jnp.sqrt(jnp.maximum(-2.0 * th / (1.0 - th), 0.0)) * (_sigmoid(gates[:, RG_WIDTH:]) * xc)
        outs = []
        for c in range(RG_WIDTH // LANES):
            sl = slice(c * LANES, (c + 1) * LANES)
            a_c = a[:, sl].reshape(groups, SUBLANES, LANES)
            u_c = u[:, sl].reshape(groups, SUBLANES, LANES)
            d = 1
            while d < SUBLANES:
                keep = sub >= d
                u_c = u_c + a_c * jnp.where(keep, pltpu.roll(u_c, d, axis=1), 0.0)
                a_c = a_c * jnp.where(keep, pltpu.roll(a_c, d, axis=1), 1.0)
                d *= 2
            h_prev = h_ref[b, 0:1, sl]
            h_groups = []
            for g in range(groups):
                h_g = u_c[g] + a_c[g] * h_prev
                h_prev = h_g[SUBLANES - 1:SUBLANES, :]
                h_groups.append(h_g)
            h_ref[b, 0:1, sl] = h_prev
            outs.append(jnp.concatenate(h_groups, axis=0))
        o_ref[b, rows, :] = (jnp.concatenate(outs, axis=-1) * _gelu_tanh(gate_ref[b, rows, :])).astype(o_ref.dtype)


def _block_diag(w):
    nb, d, _ = w.shape
    eye = jnp.eye(nb, dtype=w.dtype)
    return (eye[:, None, :, None] * w[:, :, None, :]).reshape(nb * d, nb * d)


def _rglru_mixer(gate, xr, conv_w, conv_b, wa, ba, wx, bx, lam):
    batch, seq, _ = gate.shape
    spec = _seq_spec(batch, RG_CHUNK, RG_WIDTH)
    w_gates = jnp.concatenate([_block_diag(wa.astype(F32)), _block_diag(wx.astype(F32))], axis=1).astype(BF16)
    b_gates = jnp.concatenate([ba, bx]).astype(F32).reshape(1, -1)
    return pl.pallas_call(
        _rglru_kernel,
        grid=(seq // (CHUNKS_PER_STEP * RG_CHUNK),),
        in_specs=[spec, spec, _const_spec((RG_CONV, RG_WIDTH)), _const_spec((1, RG_WIDTH)),
                  _const_spec((RG_WIDTH, 2 * RG_WIDTH)), _const_spec((1, 2 * RG_WIDTH)),
                  _const_spec((1, RG_WIDTH))],
        out_specs=spec,
        out_shape=jax.ShapeDtypeStruct((batch, seq, RG_WIDTH), BF16),
        scratch_shapes=[pltpu.VMEM((batch, SUBLANES, RG_WIDTH), F32),
                        pltpu.VMEM((batch, RG_WIDTH // LANES, RG_CHUNK + SUBLANES, LANES), F32)],
        compiler_params=_params(1),
        name="rglru_mixer",
    )(gate, xr, conv_w.astype(F32), conv_b.astype(F32).reshape(1, -1), w_gates, b_gates,
      lam.astype(F32).reshape(1, -1))


def _ffn_up(xb, wup_ref, cw_ref, cb_ref, carry_ref, h_ref, act_ref):
    tm = ROW_TILE
    halo = SUBLANES
    fc = FFN_COL_CHUNK
    for c in range(FFN_DIM // fc):
        h = [_dot(xb, wup_ref[:, part * FFN_DIM + c * fc:part * FFN_DIM + (c + 1) * fc]) for part in range(2)]
        for s in range(fc // LANES):
            conv = []
            for part in range(2):
                col = part * FFN_DIM + c * fc + s * LANES
                hb = h_ref.at[c % 2, part, s]
                hb[0:halo, :] = carry_ref[:, col:col + LANES]
                hb[halo:halo + tm, :] = h[part][:, s * LANES:(s + 1) * LANES]
                carry_ref[:, col:col + LANES] = hb[tm:tm + halo, :]
                y = cb_ref[:, col:col + LANES] + cw_ref[2:3, col:col + LANES] * hb[halo:halo + tm, :]
                for k in range(1, FFN_CONV):
                    y = y + cw_ref[2 - k:3 - k, col:col + LANES] * hb[halo - k:halo - k + tm, :]
                conv.append(y)
            act_col = c * fc + s * LANES
            act_ref[:, act_col:act_col + LANES] = (_silu(conv[0]) * conv[1]).astype(BF16)


def _tail_kernel(ya_ref, yb_ref, x_ref, wout_ref, lng_ref, lnb_ref, wup_ref, cw_ref, cb_ref, wdown_ref,
                 o_ref, carry_ref, h_ref, act_ref, x1_ref):
    @pl.when(pl.program_id(1) == 0)
    def _():
        carry_ref[...] = jnp.zeros_like(carry_ref)

    half = ya_ref.shape[1]
    m = _dot(ya_ref[...], wout_ref[0:half, :]) + _dot(yb_ref[...], wout_ref[half:2 * half, :])
    x1_ref[...] = _layer_norm(ALPHA * x_ref[...] + m, lng_ref[0:1, :], lnb_ref[0:1, :])
    _ffn_up(x1_ref[...].astype(BF16), wup_ref, cw_ref, cb_ref, carry_ref, h_ref, act_ref)
    f = _dot(act_ref[...], wdown_ref[...])
    o_ref[...] = _layer_norm(ALPHA * x1_ref[...] + f, lng_ref[1:2, :], lnb_ref[1:2, :])


def _layer_tail(ya, yb, x, w_out_all, j, ln_g_all, ln_b_all, w_up_all, conv_w_all, conv_b_all, w_down_all, layer, batch):
    rows, d = x.shape
    half = ya.shape[1]
    ntile = rows // batch // ROW_TILE
    row_spec = lambda n: pl.BlockSpec((ROW_TILE, n), lambda b, t: (b * ntile + t, 0))
    return pl.pallas_call(
        _tail_kernel,
        grid=(batch, ntile),
        in_specs=[row_spec(half), row_spec(half), row_spec(d), _layer_spec((2 * half, d), j),
                  _layer_spec((2, d), layer), _layer_spec((2, d), layer),
                  _layer_spec((d, 2 * FFN_DIM), layer), _layer_spec((FFN_CONV, 2 * FFN_DIM), layer),
                  _layer_spec((1, 2 * FFN_DIM), layer), _layer_spec((FFN_DIM, d), layer)],
        out_specs=row_spec(d),
        out_shape=jax.ShapeDtypeStruct((rows, d), F32),
        scratch_shapes=[pltpu.VMEM((SUBLANES, 2 * FFN_DIM), F32),
                        pltpu.VMEM((2, 2, FFN_COL_CHUNK // LANES, ROW_TILE + SUBLANES, LANES), F32),
                        pltpu.VMEM((ROW_TILE, FFN_DIM), BF16),
                        pltpu.VMEM((ROW_TILE, d), F32)],
        compiler_params=_params(2),
        name="layer_tail",
    )(ya, yb, x, w_out_all, ln_g_all, ln_b_all, w_up_all, conv_w_all, conv_b_all, w_down_all)


AB_SPLITS = (SSD_D_INNER, SSD_CONV_DIM, LANES, HG_WIDTH, HG_WIDTH, HG_WIDTH, HG_WIDTH)
CD_SPLITS = (SWA_Q_DIM, SWA_KV_DIM, SWA_KV_DIM, RG_WIDTH, RG_WIDTH)
AB_DTYPES = (F32, F32, F32, F32, F32, BF16, F32)
CD_DTYPES = (BF16, BF16, BF16, F32, F32)


def _ab_weight(w_in):
    a = SSD_D_INNER + SSD_CONV_DIM
    dt_cols = jnp.pad(w_in[..., a:a + SSD_HEADS], ((0, 0), (0, 0), (0, LANES - SSD_HEADS)))
    return jnp.concatenate([w_in[..., :a], dt_cols, w_in[..., a + SSD_HEADS:]], axis=-1).astype(BF16)


def kernel(x, ab_w_in, ssd_conv_w, ssd_conv_b, ssd_dt_bias, ssd_a_log, ssd_d, ssd_norm_w, hg_lower, hg_norm_w, ab_w_out, cd_w_in, swa_sinks, rg_conv_w, rg_conv_b, rg_wa, rg_ba, rg_wx, rg_bx, rg_lambda, cd_w_out, ffn_w_up, ffn_conv_w, ffn_conv_b, ffn_w_down, ln_g, ln_b):
    batch, seq, d = x.shape
    assert d == D_MODEL and seq % ROW_TILE == 0
    rows = batch * seq
    seq_view = lambda arrs: [a.reshape(batch, seq, a.shape[-1]) for a in arrs]
    xr = x.reshape(rows, d).astype(F32)
    ab_w_in_b, cd_w_in_b = _ab_weight(ab_w_in), cd_w_in.astype(BF16)
    ab_w_out_b, cd_w_out_b = ab_w_out.astype(BF16), cd_w_out.astype(BF16)
    w_up_b, w_down_b = ffn_w_up.astype(BF16), ffn_w_down.astype(BF16)
    conv_w, conv_b = ffn_conv_w.astype(F32), ffn_conv_b.astype(F32).reshape(DEPTH, 1, 2 * FFN_DIM)
    ln_g, ln_b = ln_g.astype(F32), ln_b.astype(F32)
    for layer in range(DEPTH):
        j = layer // 2
        if layer % 2 == 0:
            z, xbc, dt_raw, hq, hf, hi, hg = seq_view(_inproj(xr, ab_w_in_b, j, AB_SPLITS, AB_DTYPES))
            ya = _ssd_mixer(z, xbc, dt_raw, ssd_conv_w[j], ssd_conv_b[j], ssd_dt_bias[j], ssd_a_log[j], ssd_d[j],
                            ssd_norm_w[j])
            yb = _hgrn2_mixer(hq, hf, hi, hg, hg_lower, hg_norm_w[j], j)
            w_out_b = ab_w_out_b
        else:
            q, k, v, gate, xg = seq_view(_inproj(xr, cd_w_in_b, j, CD_SPLITS, CD_DTYPES))
            ya = _swa_attention(q, k, v, swa_sinks[j])
            yb = _rglru_mixer(gate, xg, rg_conv_w[j], rg_conv_b[j], rg_wa[j], rg_ba[j], rg_wx[j], rg_bx[j],
                              rg_lambda[j])
            w_out_b = cd_w_out_b
        xr = _layer_tail(ya.reshape(rows, -1), yb.reshape(rows, -1), xr, w_out_b, j, ln_g, ln_b,
                         w_up_b, conv_w, conv_b, w_down_b, layer, batch)
    return xr.reshape(batch, seq, d).astype(x.dtype)
```

```python
import functools
import itertools

import jax
import jax.numpy as jnp
import numpy as np
from jax import lax
from jax.experimental import pallas as pl
from jax.experimental.pallas import tpu as pltpu

F32 = jnp.float32
BF16 = jnp.bfloat16

LANES = 128
SUBLANES = 8
VMEM_LIMIT_BYTES = 56 * 1024 * 1024

D_MODEL = 1024
DEPTH = 4
SSD_HEADS = 8
SSD_HEAD_DIM = 64
SSD_D_INNER = SSD_HEADS * SSD_HEAD_DIM
SSD_GROUPS = 2
SSD_D_STATE = 64
SSD_CONV = 4
SSD_CONV_DIM = SSD_D_INNER + 2 * SSD_GROUPS * SSD_D_STATE
SSD_CHUNK = 128
HG_HEADS = 4
HG_KEY_DIM = 128
HG_VAL_DIM = 128
HG_WIDTH = HG_HEADS * HG_KEY_DIM
HG_CHUNK = 64
HG_SAFE_DECAY = 160.0
SWA_Q_HEADS = 8
SWA_KV_HEADS = 2
SWA_HEAD_DIM = 64
SWA_BLOCK = 128
SWA_Q_DIM = SWA_Q_HEADS * SWA_HEAD_DIM
SWA_KV_DIM = SWA_KV_HEADS * SWA_HEAD_DIM
RG_WIDTH = 512
RG_BLOCKS = 8
RG_BLOCK_DIM = RG_WIDTH // RG_BLOCKS
RG_CONV = 4
RG_C = 8.0
RG_CHUNK = 128
FFN_DIM = 2816
FFN_CONV = 3
FFN_COL_CHUNK = 256
ROW_TILE = 512
CHUNKS_PER_STEP = 4
INPROJ_ROW_TILE = 1024
LN_EPS = 1e-5
RMS_EPS = 1e-6
MASK_VALUE = -1e9
ALPHA = (2 * DEPTH) ** 0.25


def _params(n_axes):
    return pltpu.CompilerParams(dimension_semantics=("arbitrary",) * n_axes,
                                vmem_limit_bytes=VMEM_LIMIT_BYTES)


def _const_spec(shape):
    return pl.BlockSpec(shape, lambda *_: (0,) * len(shape), pipeline_mode=pl.Buffered(1))


def _layer_spec(shape, layer):
    return pl.BlockSpec((None,) + tuple(shape), lambda *_: (layer,) + (0,) * len(shape),
                        pipeline_mode=pl.Buffered(1))


def _sigmoid(x):
    return 1.0 / (1.0 + jnp.exp(-x))


def _silu(x):
    return x * _sigmoid(x)


def _softplus(x):
    return jnp.maximum(x, 0.0) + jnp.log1p(jnp.exp(-jnp.abs(x)))


def _gelu_tanh(x):
    return 0.5 * x * (1.0 + jnp.tanh(np.sqrt(2.0 / np.pi).astype(np.float32) * (x + 0.044715 * (x * x * x))))


def _layer_norm(r, g, b):
    mu = jnp.mean(r, axis=-1, keepdims=True)
    d = r - mu
    var = jnp.mean(d * d, axis=-1, keepdims=True)
    return d * lax.rsqrt(var + LN_EPS) * g + b


def _dot(a, b):
    return jnp.dot(a, b, preferred_element_type=F32)


def _split3(x):
    x1 = x.astype(BF16)
    r = x - x1.astype(F32)
    x2 = r.astype(BF16)
    return x1, x2, (r - x2.astype(F32)).astype(BF16)


def _dot_nt(a, b):
    return lax.dot_general(a, b, (((1,), (1,)), ((), ())), preferred_element_type=F32)


def _dot_tn(a, b):
    return lax.dot_general(a, b, (((0,), (0,)), ((), ())), preferred_element_type=F32)


def _lower_tri(n):
    row = lax.broadcasted_iota(jnp.int32, (n, n), 0)
    col = lax.broadcasted_iota(jnp.int32, (n, n), 1)
    return col <= row


def _inproj_kernel(x_ref, w_ref, *o_refs, splits):
    xb = x_ref[...].astype(BF16)
    off = 0
    for o_ref, n in zip(o_refs, splits):
        o_ref[...] = _dot(xb, w_ref[:, off:off + n]).astype(o_ref.dtype)
        off += n


def _inproj(x, w_all, layer, splits, dtypes):
    rows, d = x.shape
    n_total = sum(splits)
    return pl.pallas_call(
        functools.partial(_inproj_kernel, splits=splits),
        grid=(rows // INPROJ_ROW_TILE,),
        in_specs=[pl.BlockSpec((INPROJ_ROW_TILE, d), lambda i: (i, 0)), _layer_spec((d, n_total), layer)],
        out_specs=[pl.BlockSpec((INPROJ_ROW_TILE, n), lambda i: (i, 0)) for n in splits],
        out_shape=[jax.ShapeDtypeStruct((rows, n), dt) for n, dt in zip(splits, dtypes)],
        compiler_params=_params(1),
        name="inproj",
    )(x, w_all)


def _seq_spec(batch, t, n):
    return pl.BlockSpec((batch, CHUNKS_PER_STEP * t, n), lambda c: (0, c, 0))


def _chunks_of_step(t, batch):
    return [(c, slice(c * t, (c + 1) * t), b) for c, b in itertools.product(range(CHUNKS_PER_STEP), range(batch))]


def _causal_conv(x_new, xcat_ref, b, cw, cb, width, t):
    halo = SUBLANES
    ys = []
    for s in range(x_new.shape[1] // LANES):
        sl = slice(s * LANES, (s + 1) * LANES)
        xcat_ref[b, s, halo:halo + t, :] = x_new[:, sl]
        y = cb[:, sl] + cw[width - 1:width, sl] * x_new[:, sl]
        for k in range(1, width):
            y = y + cw[width - 1 - k:width - k, sl] * xcat_ref[b, s, halo - k:halo - k + t, :]
        xcat_ref[b, s, 0:halo, :] = xcat_ref[b, s, t:t + halo, :]
        ys.append(y)
    return jnp.concatenate(ys, axis=-1)


def _ssd_kernel(z_ref, xbc_ref, dt_ref, cw_ref, cb_ref, dtb_ref, alog_ref, dskip_ref, nw_ref, expand_ref,
                o_ref, state_ref, xcat_ref):
    t = SSD_CHUNK
    batch = z_ref.shape[0]

    @pl.when(pl.program_id(0) == 0)
    def _():
        state_ref[...] = jnp.zeros_like(state_ref)
        xcat_ref[:, :, 0:SUBLANES, :] = jnp.zeros((batch, SSD_CONV_DIM // LANES, SUBLANES, LANES), F32)

    cw = cw_ref[...]
    cb = cb_ref[...]
    neg_a = -jnp.exp(alog_ref[...])
    expand = expand_ref[...]
    tri = _lower_tri(t)
    tri3 = jnp.concatenate([jnp.where(tri, 1.0, 0.0).astype(BF16)] * 3, axis=1)
    lane = lax.broadcasted_iota(jnp.int32, (t, LANES), 1)
    lo_half = lane < SSD_D_STATE
    sub = lax.broadcasted_iota(jnp.int32, (LANES, LANES), 0)
    heads_per_group = SSD_HEADS // SSD_GROUPS
    gw = SSD_D_INNER // SSD_GROUPS

    for _, rows, b in _chunks_of_step(t, batch):
        xa = _silu(_causal_conv(xbc_ref[b, rows, :], xcat_ref, b, cw, cb, SSD_CONV, t))
        xs = xa[:, :SSD_D_INNER]
        bm = xa[:, SSD_D_INNER:SSD_D_INNER + LANES]
        cm = xa[:, SSD_D_INNER + LANES:SSD_D_INNER + 2 * LANES]

        dt = _softplus(dt_ref[b, rows, :] + dtb_ref[...])
        cs = _dot(tri3, jnp.concatenate(_split3(dt * neg_a), axis=0))
        cs_row = cs.T
        per_channel = _dot(jnp.concatenate(_split3(jnp.concatenate([dt, cs], axis=0)), axis=1), expand)
        dt_f = per_channel[:t, :]
        cs_f = per_channel[t:, :]
        cs_last = cs_f[t - 1:t, :]
        xdt = xs * dt_f
        ecs = jnp.exp(cs_f)
        w_state = (jnp.exp(cs_last - cs_f) * xdt).astype(BF16)
        dec_last = jnp.exp(cs_last)

        bm_t = bm.T.astype(BF16)
        cm_g = [jnp.where(lo_half, cm, 0.0).astype(BF16), jnp.where(lo_half, 0.0, cm).astype(BF16)]
        scores_g = [_dot(c, bm_t) for c in cm_g]

        y_pairs = []
        for pair in range(SSD_HEADS // 2):
            g = (2 * pair) // heads_per_group
            sl = slice(LANES * pair, LANES * (pair + 1))
            x_pair = xdt[:, sl]
            y_pair = None
            for j in range(2):
                h = 2 * pair + j
                x_h = jnp.where(lo_half if j == 0 else jnp.logical_not(lo_half), x_pair, 0.0).astype(BF16)
                diff = cs[:, h:h + 1] - cs_row[h:h + 1, :]
                decay = jnp.where(tri, jnp.exp(jnp.where(tri, diff, 0.0)), 0.0)
                y_h = _dot((scores_g[g] * decay).astype(BF16), x_h)
                y_pair = y_h if y_pair is None else y_pair + y_h
            state = state_ref[b, pair]
            y_pair = y_pair + _dot(cm_g[g], state.astype(BF16)) * ecs[:, sl]
            upd = _dot(bm_t, w_state[:, sl])
            in_group = (sub >= g * SSD_D_STATE) & (sub < (g + 1) * SSD_D_STATE)
            state_ref[b, pair] = jnp.where(in_group, upd, 0.0) + dec_last[:, sl] * state
            y_pairs.append(y_pair)

        y = jnp.concatenate(y_pairs, axis=-1) + dskip_ref[...] * xs
        y = y * _silu(z_ref[b, rows, :])
        outs = []
        for g in range(SSD_GROUPS):
            yg = y[:, g * gw:(g + 1) * gw]
            ms = jnp.mean(yg * yg, axis=-1, keepdims=True)
            outs.append(yg * lax.rsqrt(ms + RMS_EPS) * nw_ref[:, g * gw:(g + 1) * gw])
        o_ref[b, rows, :] = jnp.concatenate(outs, axis=-1).astype(o_ref.dtype)


def _ssd_mixer(z, xbc, dt_raw, conv_w, conv_b, dt_bias, a_log, d_skip, norm_w):
    batch, seq, _ = z.shape
    pad_heads = lambda v: jnp.pad(v.astype(F32), (0, LANES - SSD_HEADS)).reshape(1, LANES)
    expand = (jnp.arange(LANES)[:, None] == (jnp.arange(SSD_D_INNER) // SSD_HEAD_DIM)[None, :]).astype(BF16)
    expand = jnp.concatenate([expand] * 3, axis=0)
    return pl.pallas_call(
        _ssd_kernel,
        grid=(seq // (CHUNKS_PER_STEP * SSD_CHUNK),),
        in_specs=[_seq_spec(batch, SSD_CHUNK, SSD_D_INNER), _seq_spec(batch, SSD_CHUNK, SSD_CONV_DIM),
                  _seq_spec(batch, SSD_CHUNK, LANES),
                  _const_spec((SSD_CONV, SSD_CONV_DIM)), _const_spec((1, SSD_CONV_DIM)),
                  _const_spec((1, LANES)), _const_spec((1, LANES)),
                  _const_spec((1, SSD_D_INNER)), _const_spec((1, SSD_D_INNER)),
                  _const_spec((3 * LANES, SSD_D_INNER))],
        out_specs=_seq_spec(batch, SSD_CHUNK, SSD_D_INNER),
        out_shape=jax.ShapeDtypeStruct((batch, seq, SSD_D_INNER), BF16),
        scratch_shapes=[pltpu.VMEM((batch, SSD_HEADS // 2, LANES, LANES), F32),
                        pltpu.VMEM((batch, SSD_CONV_DIM // LANES, SSD_CHUNK + SUBLANES, LANES), F32)],
        compiler_params=_params(1),
        name="ssd_mixer",
    )(z, xbc, dt_raw, conv_w.astype(F32), conv_b.astype(F32).reshape(1, -1), pad_heads(dt_bias), pad_heads(a_log),
      jnp.repeat(d_skip.astype(F32), SSD_HEAD_DIM).reshape(1, -1), norm_w.astype(F32).reshape(1, -1), expand)


def _hgrn2_kernel(hq_ref, hf_ref, hi_ref, hg_ref, lower_ref, nw_ref, o_ref, state_ref, attn_ref, *, layer):
    t = HG_CHUNK
    batch = hq_ref.shape[0]
    rows = batch * t

    @pl.when(pl.program_id(0) == 0)
    def _():
        state_ref[...] = jnp.zeros_like(state_ref)

    low = lower_ref[...]
    ex = jnp.exp(low - jnp.max(low, axis=0, keepdims=True))
    sm = ex / jnp.sum(ex, axis=0, keepdims=True)
    lb = jnp.clip(jnp.sum(sm[:layer + 1, :], axis=0, keepdims=True) - sm[0:1, :], 0.0, 1.0)

    row = lax.broadcasted_iota(jnp.int32, (rows, rows), 0)
    col = lax.broadcasted_iota(jnp.int32, (rows, rows), 1)
    causal = (col <= row) & ((row & -t) == (col & -t))
    causal_b = jnp.where(causal, 1.0, 0.0).astype(BF16)
    head_slices = [slice(h * HG_KEY_DIM, (h + 1) * HG_KEY_DIM) for h in range(HG_HEADS)]

    for c in range(CHUNKS_PER_STEP):
        stack = lambda ref, c=c: jnp.concatenate([ref[b, c * t:(c + 1) * t, :] for b in range(batch)], axis=0)
        q = _silu(stack(hq_ref))
        fx = stack(hf_ref)
        e = jnp.exp(-jnp.abs(fx))
        big = 1.0 / (1.0 + e)
        small = e * big
        pos = fx >= 0.0
        log_f = jnp.log(lb + (1.0 - lb) * jnp.where(pos, big, small))
        k = (1.0 - lb) * jnp.where(pos, small, big)
        bc = _dot(jnp.concatenate([causal_b] * 3, axis=1), jnp.concatenate(_split3(log_f), axis=0))
        last_rows = [bc[(b + 1) * t - 1:(b + 1) * t, :] for b in range(batch)]
        b_last = jnp.concatenate([jnp.broadcast_to(r, (t, HG_WIDTH)) for r in last_rows], axis=0)

        mid = 0.5 * b_last
        q_mid = (q * jnp.exp(bc - mid)).astype(BF16)
        k_mid = (k * jnp.exp(mid - bc)).astype(BF16)
        for h, sl in enumerate(head_slices):
            attn_ref[h] = _dot_nt(q_mid[:, sl], k_mid[:, sl])

        @pl.when(jnp.logical_not(jnp.max(-jnp.concatenate(last_rows, axis=0)) <= HG_SAFE_DECAY))
        def _():
            col_id = lax.broadcasted_iota(jnp.int32, (rows, rows), 1)
            row_id = lax.broadcasted_iota(jnp.int32, (rows, HG_KEY_DIM), 0)
            for h, sl in enumerate(head_slices):

                def column(s, acc, q_h=q[:, sl], k_h=k[:, sl], bc_h=bc[:, sl]):
                    k_s = jnp.sum(jnp.where(row_id == s, k_h, 0.0), axis=0, keepdims=True)
                    bc_s = jnp.sum(jnp.where(row_id == s, bc_h, 0.0), axis=0, keepdims=True)
                    w = q_h * k_s * jnp.exp(jnp.minimum(bc_h - bc_s, 0.0))
                    return jnp.where(col_id == s, jnp.sum(w, axis=-1, keepdims=True), acc)

                attn_ref[h] = lax.fori_loop(0, rows, column, jnp.zeros((rows, rows), F32))

        vb = stack(hi_ref)
        q_state = (q * jnp.exp(bc)).astype(BF16)
        k_state = (k * jnp.exp(b_last - bc)).astype(BF16)
        outs = []
        for h, sl in enumerate(head_slices):
            o = _dot(jnp.where(causal, attn_ref[h], 0.0).astype(BF16), vb[:, sl])
            o_inter = []
            for b in range(batch):
                rs = slice(b * t, (b + 1) * t)
                state = state_ref[b, h]
                o_inter.append(_dot_nt(q_state[rs, sl], state.astype(BF16)))
                state_ref[b, h] = (state * jnp.exp(last_rows[b][:, sl])
                                   + _dot_tn(vb[rs, sl], k_state[rs, sl]))
            o = o + jnp.concatenate(o_inter, axis=0)
            ms = jnp.mean(o * o, axis=-1, keepdims=True)
            outs.append(o * lax.rsqrt(ms + RMS_EPS) * nw_ref[...])
        y = jnp.concatenate(outs, axis=-1) * _silu(stack(hg_ref))
        for b in range(batch):
            o_ref[b, c * t:(c + 1) * t, :] = y[b * t:(b + 1) * t, :].astype(o_ref.dtype)


def _hgrn2_mixer(hq, hf, hi, hg, hg_lower, norm_w, layer):
    batch, seq, _ = hq.shape
    spec = _seq_spec(batch, HG_CHUNK, HG_WIDTH)
    return pl.pallas_call(
        functools.partial(_hgrn2_kernel, layer=layer),
        grid=(seq // (CHUNKS_PER_STEP * HG_CHUNK),),
        in_specs=[spec, spec, spec, spec, _const_spec(hg_lower.shape), _const_spec((1, HG_VAL_DIM))],
        out_specs=spec,
        out_shape=jax.ShapeDtypeStruct((batch, seq, HG_WIDTH), BF16),
        scratch_shapes=[pltpu.VMEM((batch, HG_HEADS, HG_VAL_DIM, HG_KEY_DIM), F32),
                        pltpu.VMEM((HG_HEADS, batch * HG_CHUNK, batch * HG_CHUNK), F32)],
        compiler_params=_params(1),
        name="hgrn2_mixer",
    )(hq, hf, hi, hg, hg_lower.astype(F32), norm_w.astype(F32).reshape(1, -1))


def _swa_kernel(sinks_ref, q_ref, kp_ref, kc_ref, vp_ref, vc_ref, o_ref):
    t = SWA_BLOCK
    batch = q_ref.shape[0]
    group = SWA_Q_HEADS // SWA_KV_HEADS
    rows = group * t

    lane_kv = lax.broadcasted_iota(jnp.int32, (2 * t, LANES), 1)
    lo_kv = lane_kv < SWA_HEAD_DIM
    lane_q = lax.broadcasted_iota(jnp.int32, (t, LANES), 1)
    lo_q = lane_q < SWA_HEAD_DIM
    row = lax.broadcasted_iota(jnp.int32, (rows, 2 * t), 0) & (t - 1)
    col = lax.broadcasted_iota(jnp.int32, (rows, 2 * t), 1)
    band = (col > row) & (col <= row + t)
    head_of_row = lax.broadcasted_iota(jnp.int32, (rows, 1), 0) // t

    for c, qrows, b in _chunks_of_step(t, batch):
        q = q_ref[b, qrows, :].astype(F32) * (SWA_HEAD_DIM ** -0.5)
        if c == 0:
            visible = band & ((col >= t) | (pl.program_id(0) > 0))
            kcat = jnp.concatenate([kp_ref[b], kc_ref[b, 0:t, :]], axis=0).astype(F32)
            vcat = jnp.concatenate([vp_ref[b], vc_ref[b, 0:t, :]], axis=0).astype(F32)
        else:
            visible = band
            kcat = kc_ref[b, (c - 1) * t:(c + 1) * t, :].astype(F32)
            vcat = vc_ref[b, (c - 1) * t:(c + 1) * t, :].astype(F32)
        k_swap = pltpu.roll(kcat, SWA_HEAD_DIM, axis=1)
        v_swap = pltpu.roll(vcat, SWA_HEAD_DIM, axis=1)
        outs = []
        for g in range(SWA_KV_HEADS):
            k_dup = (jnp.where(lo_kv, kcat, k_swap) if g == 0 else jnp.where(lo_kv, k_swap, kcat)).astype(BF16)
            v_dup = (jnp.where(lo_kv, vcat, v_swap) if g == 0 else jnp.where(lo_kv, v_swap, vcat)).astype(BF16)
            pieces = []
            sink = jnp.zeros((rows, 1), F32)
            for hh in range(group):
                head = g * group + hh
                q_pair = q[:, (head // 2) * LANES:(head // 2 + 1) * LANES]
                pieces.append(jnp.where(lo_q if head % 2 == 0 else jnp.logical_not(lo_q), q_pair, 0.0))
                sink = jnp.where(head_of_row == hh, sinks_ref[head], sink)
            s = _dot_nt(jnp.concatenate(pieces, axis=0).astype(BF16), k_dup)
            s = jnp.where(visible, s, MASK_VALUE)
            m = jnp.maximum(jnp.max(s, axis=-1, keepdims=True), sink)
            p = jnp.exp(s - m)
            denom = jnp.sum(p, axis=-1, keepdims=True) + jnp.exp(sink - m)
            o = _dot((p / denom).astype(BF16), v_dup)
            for pp in range(group // 2):
                outs.append(jnp.where(lo_q, o[2 * pp * t:(2 * pp + 1) * t, :], o[(2 * pp + 1) * t:(2 * pp + 2) * t, :]))
        o_ref[b, qrows, :] = jnp.concatenate(outs, axis=-1).astype(o_ref.dtype)


def _swa_attention(q, k, v, sinks):
    batch, seq, _ = q.shape
    cur = lambda n: pl.BlockSpec((batch, CHUNKS_PER_STEP * SWA_BLOCK, n), lambda i: (0, i, 0))
    prev = lambda n: pl.BlockSpec((batch, SWA_BLOCK, n), lambda i: (0, jnp.maximum(CHUNKS_PER_STEP * i - 1, 0), 0))
    return pl.pallas_call(
        _swa_kernel,
        grid=(seq // (CHUNKS_PER_STEP * SWA_BLOCK),),
        in_specs=[pl.BlockSpec(memory_space=pltpu.SMEM), cur(SWA_Q_DIM),
                  prev(SWA_KV_DIM), cur(SWA_KV_DIM), prev(SWA_KV_DIM), cur(SWA_KV_DIM)],
        out_specs=cur(SWA_Q_DIM),
        out_shape=jax.ShapeDtypeStruct((batch, seq, SWA_Q_DIM), BF16),
        compiler_params=_params(1),
        name="swa_attention",
    )(sinks.astype(F32), q, k, k, v, v)


def _rglru_kernel(gate_ref, xr_ref, cw_ref, cb_ref, wg_ref, bg_ref, lam_ref, o_ref, h_ref, xcat_ref):
    t = RG_CHUNK
    batch = gate_ref.shape[0]
    groups = t // SUBLANES

    @pl.when(pl.program_id(0) == 0)
    def _():
        h_ref[...] = jnp.zeros_like(h_ref)
        xcat_ref[:, :, 0:SUBLANES, :] = jnp.zeros((batch, RG_WIDTH // LANES, SUBLANES, LANES), F32)

    cw = cw_ref[...]
    cb = cb_ref[...]
    decay_rate = (-RG_C) * _softplus(-lam_ref[...])
    sub = lax.broadcasted_iota(jnp.int32, (groups, SUBLANES, LANES), 1)
    for _, rows, b in _chunks_of_step(t, batch):
        xc = _causal_conv(xr_ref[b, rows, :], xcat_ref, b, cw, cb, RG_CONV, t)
        gates = _dot(xc.astype(BF16), wg_ref[...]) + bg_ref[...]
        log_a = decay_rate * _sigmoid(gates[:, :RG_WIDTH])
        a = jnp.exp(log_a)
        th = jnp.tanh(log_a)
        u = jnp.sqrt(jnp.maximum(-2.0 * th / (1.0 - th), 0.0)) * (_sigmoid(gates[:, RG_WIDTH:]) * xc)
        outs = []
        for c in range(RG_WIDTH // LANES):
            sl = slice(c * LANES, (c + 1) * LANES)
            a_c = a[:, sl].reshape(groups, SUBLANES, LANES)
            u_c = u[:, sl].reshape(groups, SUBLANES, LANES)
            d = 1
            while d < SUBLANES:
                keep = sub >= d
                u_c = u_c + a_c * jnp.where(keep, pltpu.roll(u_c, d, axis=1), 0.0)
                a_c = a_c * jnp.where(keep, pltpu.roll(a_c, d, axis=1), 1.0)
                d *= 2
            h_prev = h_ref[b, 0:1, sl]
            h_groups = []
            for g in range(groups):
                h_g = u_c[g] + a_c[g] * h_prev
                h_prev = h_g[SUBLANES - 1:SUBLANES, :]
                h_groups.append(h_g)
            h_ref[b, 0:1, sl] = h_prev
            outs.append(jnp.concatenate(h_groups, axis=0))
        o_ref[b, rows, :] = (jnp.concatenate(outs, axis=-1) * _gelu_tanh(gate_ref[b, rows, :])).astype(o_ref.dtype)


def _block_diag(w):
    nb, d, _ = w.shape
    eye = jnp.eye(nb, dtype=w.dtype)
    return (eye[:, None, :, None] * w[:, :, None, :]).reshape(nb * d, nb * d)


def _rglru_mixer(gate, xr, conv_w, conv_b, wa, ba, wx, bx, lam):
    batch, seq, _ = gate.shape
    spec = _seq_spec(batch, RG_CHUNK, RG_WIDTH)
    w_gates = jnp.concatenate([_block_diag(wa.astype(F32)), _block_diag(wx.astype(F32))], axis=1).astype(BF16)
    b_gates = jnp.concatenate([ba, bx]).astype(F32).reshape(1, -1)
    return pl.pallas_call(
        _rglru_kernel,
        grid=(seq // (CHUNKS_PER_STEP * RG_CHUNK),),
        in_specs=[spec, spec, _const_spec((RG_CONV, RG_WIDTH)), _const_spec((1, RG_WIDTH)),
                  _const_spec((RG_WIDTH, 2 * RG_WIDTH)), _const_spec((1, 2 * RG_WIDTH)),
                  _const_spec((1, RG_WIDTH))],
        out_specs=spec,
        out_shape=jax.ShapeDtypeStruct((batch, seq, RG_WIDTH), BF16),
        scratch_shapes=[pltpu.VMEM((batch, SUBLANES, RG_WIDTH), F32),
                        pltpu.VMEM((batch, RG_WIDTH // LANES, RG_CHUNK + SUBLANES, LANES), F32)],
        compiler_params=_params(1),
        name="rglru_mixer",
    )(gate, xr, conv_w.astype(F32), conv_b.astype(F32).reshape(1, -1), w_gates, b_gates,
      lam.astype(F32).reshape(1, -1))


def _ffn_up(xb, wup_ref, cw_ref, cb_ref, carry_ref, h_ref, act_ref):
    tm = ROW_TILE
    halo = SUBLANES
    fc = FFN_COL_CHUNK
    for c in range(FFN_DIM // fc):
        h = [_dot(xb, wup_ref[:, part * FFN_DIM + c * fc:part * FFN_DIM + (c + 1) * fc]) for part in range(2)]
        for s in range(fc // LANES):
            conv = []
            for part in range(2):
                col = part * FFN_DIM + c * fc + s * LANES
                hb = h_ref.at[c % 2, part, s]
                hb[0:halo, :] = carry_ref[:, col:col + LANES]
                hb[halo:halo + tm, :] = h[part][:, s * LANES:(s + 1) * LANES]
                carry_ref[:, col:col + LANES] = hb[tm:tm + halo, :]
                y = cb_ref[:, col:col + LANES] + cw_ref[2:3, col:col + LANES] * hb[halo:halo + tm, :]
                for k in range(1, FFN_CONV):
                    y = y + cw_ref[2 - k:3 - k, col:col + LANES] * hb[halo - k:halo - k + tm, :]
                conv.append(y)
            act_col = c * fc + s * LANES
            act_ref[:, act_col:act_col + LANES] = (_silu(conv[0]) * conv[1]).astype(BF16)


def _tail_kernel(ya0_ref, yb0_ref, x0_ref, yan_ref, ybn_ref, xn_ref, wout_ref, lng_ref, lnb_ref, wup_ref, cw_ref,
                 cb_ref, wdown_ref, o_ref, carry_ref, h_ref, act_ref, x1_ref, *, tiles_per_seq):
    j = pl.program_id(0)

    def head(ya_ref, yb_ref, x_ref):
        half = ya_ref.shape[1]
        m = _dot(ya_ref[...], wout_ref[0:half, :]) + _dot(yb_ref[...], wout_ref[half:2 * half, :])
        return _layer_norm(ALPHA * x_ref[...] + m, lng_ref[0:1, :], lnb_ref[0:1, :])

    @pl.when(j == 0)
    def _():
        x1_ref[0] = head(ya0_ref, yb0_ref, x0_ref)

    @pl.when(lax.rem(j, tiles_per_seq) == 0)
    def _():
        carry_ref[...] = jnp.zeros_like(carry_ref)

    def step(cur, nxt):
        _ffn_up(x1_ref[cur].astype(BF16), wup_ref, cw_ref, cb_ref, carry_ref, h_ref, act_ref)
        x1_ref[nxt] = head(yan_ref, ybn_ref, xn_ref)
        f = _dot(act_ref[...], wdown_ref[...])
        o_ref[...] = _layer_norm(ALPHA * x1_ref[cur] + f, lng_ref[1:2, :], lnb_ref[1:2, :])

    @pl.when(lax.rem(j, 2) == 0)
    def _():
        step(0, 1)

    @pl.when(lax.rem(j, 2) == 1)
    def _():
        step(1, 0)


def _layer_tail(ya, yb, x, w_out_all, j, ln_g_all, ln_b_all, w_up_all, conv_w_all, conv_b_all, w_down_all, layer, seq):
    rows, d = x.shape
    half = ya.shape[1]
    ntile = rows // ROW_TILE
    first = lambda n: pl.BlockSpec((ROW_TILE, n), lambda t: (0, 0), pipeline_mode=pl.Buffered(1))
    nxt = lambda n: pl.BlockSpec((ROW_TILE, n), lambda t: (jnp.minimum(t + 1, ntile - 1), 0))
    return pl.pallas_call(
        functools.partial(_tail_kernel, tiles_per_seq=seq // ROW_TILE),
        grid=(ntile,),
        in_specs=[first(half), first(half), first(d), nxt(half), nxt(half), nxt(d),
                  _layer_spec((2 * half, d), j), _layer_spec((2, d), layer), _layer_spec((2, d), layer),
                  _layer_spec((d, 2 * FFN_DIM), layer), _layer_spec((FFN_CONV, 2 * FFN_DIM), layer),
                  _layer_spec((1, 2 * FFN_DIM), layer), _layer_spec((FFN_DIM, d), layer)],
        out_specs=pl.BlockSpec((ROW_TILE, d), lambda t: (t, 0)),
        out_shape=jax.ShapeDtypeStruct((rows, d), F32),
        scratch_shapes=[pltpu.VMEM((SUBLANES, 2 * FFN_DIM), F32),
                        pltpu.VMEM((2, 2, FFN_COL_CHUNK // LANES, ROW_TILE + SUBLANES, LANES), F32),
                        pltpu.VMEM((ROW_TILE, FFN_DIM), BF16),
                        pltpu.VMEM((2, ROW_TILE, d), F32)],
        compiler_params=_params(1),
        name="layer_tail",
    )(ya, yb, x, ya, yb, x, w_out_all, ln_g_all, ln_b_all, w_up_all, conv_w_all, conv_b_all, w_down_all)


AB_SPLITS = (SSD_D_INNER, SSD_CONV_DIM, LANES, HG_WIDTH, HG_WIDTH, HG_WIDTH, HG_WIDTH)
CD_SPLITS = (SWA_Q_DIM, SWA_KV_DIM, SWA_KV_DIM, RG_WIDTH, RG_WIDTH)
AB_DTYPES = (F32, F32, F32, F32, F32, BF16, F32)
CD_DTYPES = (BF16, BF16, BF16, F32, F32)


def _ab_weight(w_in):
    a = SSD_D_INNER + SSD_CONV_DIM
    dt_cols = jnp.pad(w_in[..., a:a + SSD_HEADS], ((0, 0), (0, 0), (0, LANES - SSD_HEADS)))
    return jnp.concatenate([w_in[..., :a], dt_cols, w_in[..., a + SSD_HEADS:]], axis=-1).astype(BF16)


def kernel(x, ab_w_in, ssd_conv_w, ssd_conv_b, ssd_dt_bias, ssd_a_log, ssd_d, ssd_norm_w, hg_lower, hg_norm_w, ab_w_out, cd_w_in, swa_sinks, rg_conv_w, rg_conv_b, rg_wa, rg_ba, rg_wx, rg_bx, rg_lambda, cd_w_out, ffn_w_up, ffn_conv_w, ffn_conv_b, ffn_w_down, ln_g, ln_b):
    batch, seq, d = x.shape
    assert d == D_MODEL and seq % ROW_TILE == 0
    rows = batch * seq
    seq_view = lambda arrs: [a.reshape(batch, seq, a.shape[-1]) for a in arrs]
    xr = x.reshape(rows, d).astype(F32)
    ab_w_in_b, cd_w_in_b = _ab_weight(ab_w_in), cd_w_in.astype(BF16)
    ab_w_out_b, cd_w_out_b = ab_w_out.astype(BF16), cd_w_out.astype(BF16)
    w_up_b, w_down_b = ffn_w_up.astype(BF16), ffn_w_down.astype(BF16)
    conv_w, conv_b = ffn_conv_w.astype(F32), ffn_conv_b.astype(F32).reshape(DEPTH, 1, 2 * FFN_DIM)
    ln_g, ln_b = ln_g.astype(F32), ln_b.astype(F32)
    for layer in range(DEPTH):
        j = layer // 2
        if layer % 2 == 0:
            z, xbc, dt_raw, hq, hf, hi, hg = seq_view(_inproj(xr, ab_w_in_b, j, AB_SPLITS, AB_DTYPES))
            ya = _ssd_mixer(z, xbc, dt_raw, ssd_conv_w[j], ssd_conv_b[j], ssd_dt_bias[j], ssd_a_log[j], ssd_d[j],
                            ssd_norm_w[j])
            yb = _hgrn2_mixer(hq, hf, hi, hg, hg_lower, hg_norm_w[j], j)
            w_out_b = ab_w_out_b
        else:
            q, k, v, gate, xg = seq_view(_inproj(xr, cd_w_in_b, j, CD_SPLITS, CD_DTYPES))
            ya = _swa_attention(q, k, v, swa_sinks[j])
            yb = _rglru_mixer(gate, xg, rg_conv_w[j], rg_conv_b[j], rg_wa[j], rg_ba[j], rg_wx[j], rg_bx[j],
                              rg_lambda[j])
            w_out_b = cd_w_out_b
        xr = _layer_tail(ya.reshape(rows, -1), yb.reshape(rows, -1), xr, w_out_b, j, ln_g, ln_b,
                         w_up_b, conv_w, conv_b, w_down_b, layer, seq)
    return xr.reshape(batch, seq, d).astype(x.dtype)
```

```python
import functools
import itertools

import jax
import jax.numpy as jnp
import numpy as np
from jax import lax
from jax.experimental import pallas as pl
from jax.experimental.pallas import tpu as pltpu

F32 = jnp.float32
BF16 = jnp.bfloat16

LANES = 128
SUBLANES = 8
VMEM_LIMIT_BYTES = 56 * 1024 * 1024

D_MODEL = 1024
DEPTH = 4
SSD_HEADS = 8
SSD_HEAD_DIM = 64
SSD_D_INNER = SSD_HEADS * SSD_HEAD_DIM
SSD_GROUPS = 2
SSD_D_STATE = 64
SSD_CONV = 4
SSD_CONV_DIM = SSD_D_INNER + 2 * SSD_GROUPS * SSD_D_STATE
SSD_CHUNK = 128
SSD_HEAD_ROWS = 16
HG_HEADS = 4
HG_KEY_DIM = 128
HG_VAL_DIM = 128
HG_WIDTH = HG_HEADS * HG_KEY_DIM
HG_CHUNK = 64
HG_SAFE_DECAY = 160.0
SWA_Q_HEADS = 8
SWA_KV_HEADS = 2
SWA_HEAD_DIM = 64
SWA_BLOCK = 128
SWA_Q_DIM = SWA_Q_HEADS * SWA_HEAD_DIM
SWA_KV_DIM = SWA_KV_HEADS * SWA_HEAD_DIM
RG_WIDTH = 512
RG_BLOCKS = 8
RG_BLOCK_DIM = RG_WIDTH // RG_BLOCKS
RG_CONV = 4
RG_C = 8.0
RG_CHUNK = 128
FFN_DIM = 2816
FFN_CONV = 3
FFN_COL_CHUNK = 256
ROW_TILE = 512
CHUNKS_PER_STEP = 4
INPROJ_ROW_TILE = 1024
LN_EPS = 1e-5
RMS_EPS = 1e-6
MASK_VALUE = -1e9
ALPHA = (2 * DEPTH) ** 0.25


def _params(n_axes):
    return pltpu.CompilerParams(dimension_semantics=("arbitrary",) * n_axes,
                                vmem_limit_bytes=VMEM_LIMIT_BYTES)


def _const_spec(shape):
    return pl.BlockSpec(shape, lambda *_: (0,) * len(shape), pipeline_mode=pl.Buffered(1))


def _layer_spec(shape, layer):
    return pl.BlockSpec((None,) + tuple(shape), lambda *_: (layer,) + (0,) * len(shape),
                        pipeline_mode=pl.Buffered(1))


def _sigmoid(x):
    return 1.0 / (1.0 + jnp.exp(-x))


def _silu(x):
    return x * _sigmoid(x)


def _softplus(x):
    return jnp.maximum(x, 0.0) + jnp.log1p(jnp.exp(-jnp.abs(x)))


def _gelu_tanh(x):
    return 0.5 * x * (1.0 + jnp.tanh(np.sqrt(2.0 / np.pi).astype(np.float32) * (x + 0.044715 * (x * x * x))))


def _layer_norm(r, g, b):
    mu = jnp.mean(r, axis=-1, keepdims=True)
    d = r - mu
    var = jnp.mean(d * d, axis=-1, keepdims=True)
    return d * lax.rsqrt(var + LN_EPS) * g + b


def _dot(a, b):
    return jnp.dot(a, b, preferred_element_type=F32)


def _split3(x):
    x1 = x.astype(BF16)
    r = x - x1.astype(F32)
    x2 = r.astype(BF16)
    return x1, x2, (r - x2.astype(F32)).astype(BF16)


def _dot_nt(a, b):
    return lax.dot_general(a, b, (((1,), (1,)), ((), ())), preferred_element_type=F32)


def _dot_tn(a, b):
    return lax.dot_general(a, b, (((0,), (0,)), ((), ())), preferred_element_type=F32)


def _lower_tri(n):
    row = lax.broadcasted_iota(jnp.int32, (n, n), 0)
    col = lax.broadcasted_iota(jnp.int32, (n, n), 1)
    return col <= row


def _inproj_kernel(x_ref, w_ref, *o_refs, splits):
    xb = x_ref[...].astype(BF16)
    off = 0
    for o_ref, n in zip(o_refs, splits):
        o_ref[...] = _dot(xb, w_ref[:, off:off + n]).astype(o_ref.dtype)
        off += n


def _inproj(x, w_all, layer, splits, dtypes):
    rows, d = x.shape
    n_total = sum(splits)
    return pl.pallas_call(
        functools.partial(_inproj_kernel, splits=splits),
        grid=(rows // INPROJ_ROW_TILE,),
        in_specs=[pl.BlockSpec((INPROJ_ROW_TILE, d), lambda i: (i, 0)), _layer_spec((d, n_total), layer)],
        out_specs=[pl.BlockSpec((INPROJ_ROW_TILE, n), lambda i: (i, 0)) for n in splits],
        out_shape=[jax.ShapeDtypeStruct((rows, n), dt) for n, dt in zip(splits, dtypes)],
        compiler_params=_params(1),
        name="inproj",
    )(x, w_all)


def _seq_spec(batch, t, n):
    return pl.BlockSpec((batch, CHUNKS_PER_STEP * t, n), lambda c: (0, c, 0))


def _chunks_of_step(t, batch):
    return [(c, slice(c * t, (c + 1) * t), b) for c, b in itertools.product(range(CHUNKS_PER_STEP), range(batch))]


def _causal_conv(x_new, xcat_ref, b, cw, cb, width, t):
    halo = SUBLANES
    ys = []
    for s in range(x_new.shape[1] // LANES):
        sl = slice(s * LANES, (s + 1) * LANES)
        xcat_ref[b, s, halo:halo + t, :] = x_new[:, sl]
        y = cb[:, sl] + cw[width - 1:width, sl] * x_new[:, sl]
        for k in range(1, width):
            y = y + cw[width - 1 - k:width - k, sl] * xcat_ref[b, s, halo - k:halo - k + t, :]
        xcat_ref[b, s, 0:halo, :] = xcat_ref[b, s, t:t + halo, :]
        ys.append(y)
    return jnp.concatenate(ys, axis=-1)


def _ssd_kernel(z_ref, xbc_ref, dt_ref, cw_ref, cb_ref, dtb_ref, alog_ref, dskip_ref, nw_ref, expand_ref,
                o_ref, state_ref, xcat_ref):
    t = SSD_CHUNK
    batch = z_ref.shape[0]

    @pl.when(pl.program_id(0) == 0)
    def _():
        state_ref[...] = jnp.zeros_like(state_ref)
        xcat_ref[:, :, 0:SUBLANES, :] = jnp.zeros((batch, SSD_CONV_DIM // LANES, SUBLANES, LANES), F32)

    cw = cw_ref[...]
    cb = cb_ref[...]
    neg_a = -jnp.exp(alog_ref[...])
    expand = expand_ref[...]
    tri = _lower_tri(t)
    upper = lax.broadcasted_iota(jnp.int32, (t, t), 0) <= lax.broadcasted_iota(jnp.int32, (t, t), 1)
    triu3 = jnp.concatenate([jnp.where(upper, 1.0, 0.0).astype(BF16)] * 3, axis=0)
    pad_rows = jnp.zeros((LANES - SSD_HEAD_ROWS, t), F32)
    lane = lax.broadcasted_iota(jnp.int32, (t, LANES), 1)
    lo_half = lane < SSD_D_STATE
    sub = lax.broadcasted_iota(jnp.int32, (LANES, LANES), 0)
    heads_per_group = SSD_HEADS // SSD_GROUPS
    gw = SSD_D_INNER // SSD_GROUPS

    for _, rows, b in _chunks_of_step(t, batch):
        xa = _silu(_causal_conv(xbc_ref[b, rows, :], xcat_ref, b, cw, cb, SSD_CONV, t))
        xs = xa[:, :SSD_D_INNER]
        bm = xa[:, SSD_D_INNER:SSD_D_INNER + LANES]
        cm = xa[:, SSD_D_INNER + LANES:SSD_D_INNER + 2 * LANES]

        dt_t = _softplus((dt_ref[b, rows, :] + dtb_ref[...]).T[:SSD_HEAD_ROWS, :])
        cs_row = _dot(jnp.concatenate(_split3(dt_t * neg_a), axis=1), triu3)
        cs = jnp.concatenate([cs_row, pad_rows], axis=0).T
        per_channel = _dot_tn(jnp.concatenate(_split3(jnp.concatenate([dt_t, cs_row], axis=1)), axis=0), expand)
        dt_f = per_channel[:t, :]
        cs_f = per_channel[t:, :]
        cs_last = cs_f[t - 1:t, :]
        xdt = xs * dt_f
        ecs = jnp.exp(cs_f)
        w_state = (jnp.exp(cs_last - cs_f) * xdt).astype(BF16)
        dec_last = jnp.exp(cs_last)

        bm_t = bm.T.astype(BF16)
        cm_g = [jnp.where(lo_half, cm, 0.0).astype(BF16), jnp.where(lo_half, 0.0, cm).astype(BF16)]
        scores_g = [_dot(c, bm_t) for c in cm_g]

        y_pairs = []
        for pair in range(SSD_HEADS // 2):
            g = (2 * pair) // heads_per_group
            sl = slice(LANES * pair, LANES * (pair + 1))
            x_pair = xdt[:, sl]
            y_pair = None
            for j in range(2):
                h = 2 * pair + j
                x_h = jnp.where(lo_half if j == 0 else jnp.logical_not(lo_half), x_pair, 0.0).astype(BF16)
                diff = cs[:, h:h + 1] - cs_row[h:h + 1, :]
                decay = jnp.where(tri, jnp.exp(jnp.where(tri, diff, 0.0)), 0.0)
                y_h = _dot((scores_g[g] * decay).astype(BF16), x_h)
                y_pair = y_h if y_pair is None else y_pair + y_h
            state = state_ref[b, pair]
            y_pair = y_pair + _dot(cm_g[g], state.astype(BF16)) * ecs[:, sl]
            upd = _dot(bm_t, w_state[:, sl])
            in_group = (sub >= g * SSD_D_STATE) & (sub < (g + 1) * SSD_D_STATE)
            state_ref[b, pair] = jnp.where(in_group, upd, 0.0) + dec_last[:, sl] * state
            y_pairs.append(y_pair)

        y = jnp.concatenate(y_pairs, axis=-1) + dskip_ref[...] * xs
        y = y * _silu(z_ref[b, rows, :])
        outs = []
        for g in range(SSD_GROUPS):
            yg = y[:, g * gw:(g + 1) * gw]
            ms = jnp.mean(yg * yg, axis=-1, keepdims=True)
            outs.append(yg * lax.rsqrt(ms + RMS_EPS) * nw_ref[:, g * gw:(g + 1) * gw])
        o_ref[b, rows, :] = jnp.concatenate(outs, axis=-1).astype(o_ref.dtype)


def _ssd_mixer(z, xbc, dt_raw, conv_w, conv_b, dt_bias, a_log, d_skip, norm_w):
    batch, seq, _ = z.shape
    dt_bias_lanes = jnp.pad(dt_bias.astype(F32), (0, LANES - SSD_HEADS)).reshape(1, LANES)
    a_log_rows = jnp.broadcast_to(jnp.pad(a_log.astype(F32), (0, SSD_HEAD_ROWS - SSD_HEADS))[:, None],
                                  (SSD_HEAD_ROWS, SSD_CHUNK))
    expand = (jnp.arange(SSD_HEAD_ROWS)[:, None] == (jnp.arange(SSD_D_INNER) // SSD_HEAD_DIM)[None, :]).astype(BF16)
    expand = jnp.concatenate([expand] * 3, axis=0)
    return pl.pallas_call(
        _ssd_kernel,
        grid=(seq // (CHUNKS_PER_STEP * SSD_CHUNK),),
        in_specs=[_seq_spec(batch, SSD_CHUNK, SSD_D_INNER), _seq_spec(batch, SSD_CHUNK, SSD_CONV_DIM),
                  _seq_spec(batch, SSD_CHUNK, LANES),
                  _const_spec((SSD_CONV, SSD_CONV_DIM)), _const_spec((1, SSD_CONV_DIM)),
                  _const_spec((1, LANES)), _const_spec((SSD_HEAD_ROWS, SSD_CHUNK)),
                  _const_spec((1, SSD_D_INNER)), _const_spec((1, SSD_D_INNER)),
                  _const_spec((3 * SSD_HEAD_ROWS, SSD_D_INNER))],
        out_specs=_seq_spec(batch, SSD_CHUNK, SSD_D_INNER),
        out_shape=jax.ShapeDtypeStruct((batch, seq, SSD_D_INNER), BF16),
        scratch_shapes=[pltpu.VMEM((batch, SSD_HEADS // 2, LANES, LANES), F32),
                        pltpu.VMEM((batch, SSD_CONV_DIM // LANES, SSD_CHUNK + SUBLANES, LANES), F32)],
        compiler_params=_params(1),
        name="ssd_mixer",
    )(z, xbc, dt_raw, conv_w.astype(F32), conv_b.astype(F32).reshape(1, -1), dt_bias_lanes, a_log_rows,
      jnp.repeat(d_skip.astype(F32), SSD_HEAD_DIM).reshape(1, -1), norm_w.astype(F32).reshape(1, -1), expand)


def _hgrn2_kernel(hq_ref, hf_ref, hi_ref, hg_ref, lower_ref, nw_ref, o_ref, state_ref, attn_ref, *, layer):
    t = HG_CHUNK
    batch = hq_ref.shape[0]
    rows = batch * t

    @pl.when(pl.program_id(0) == 0)
    def _():
        state_ref[...] = jnp.zeros_like(state_ref)

    low = lower_ref[...]
    ex = jnp.exp(low - jnp.max(low, axis=0, keepdims=True))
    sm = ex / jnp.sum(ex, axis=0, keepdims=True)
    lb = jnp.clip(jnp.sum(sm[:layer + 1, :], axis=0, keepdims=True) - sm[0:1, :], 0.0, 1.0)

    row = lax.broadcasted_iota(jnp.int32, (rows, rows), 0)
    col = lax.broadcasted_iota(jnp.int32, (rows, rows), 1)
    causal = (col <= row) & ((row & -t) == (col & -t))
    causal_b = jnp.where(causal, 1.0, 0.0).astype(BF16)
    head_slices = [slice(h * HG_KEY_DIM, (h + 1) * HG_KEY_DIM) for h in range(HG_HEADS)]

    for c in range(CHUNKS_PER_STEP):
        stack = lambda ref, c=c: jnp.concatenate([ref[b, c * t:(c + 1) * t, :] for b in range(batch)], axis=0)
        q = _silu(stack(hq_ref))
        fx = stack(hf_ref)
        e = jnp.exp(-jnp.abs(fx))
        big = 1.0 / (1.0 + e)
        small = e * big
        pos = fx >= 0.0
        log_f = jnp.log(lb + (1.0 - lb) * jnp.where(pos, big, small))
        k = (1.0 - lb) * jnp.where(pos, small, big)
        bc = _dot(jnp.concatenate([causal_b] * 3, axis=1), jnp.concatenate(_split3(log_f), axis=0))
        last_rows = [bc[(b + 1) * t - 1:(b + 1) * t, :] for b in range(batch)]
        b_last = jnp.concatenate([jnp.broadcast_to(r, (t, HG_WIDTH)) for r in last_rows], axis=0)

        mid = 0.5 * b_last
        q_mid = (q * jnp.exp(bc - mid)).astype(BF16)
        k_mid = (k * jnp.exp(mid - bc)).astype(BF16)
        for h, sl in enumerate(head_slices):
            attn_ref[h] = _dot_nt(q_mid[:, sl], k_mid[:, sl])

        @pl.when(jnp.logical_not(jnp.max(-jnp.concatenate(last_rows, axis=0)) <= HG_SAFE_DECAY))
        def _():
            col_id = lax.broadcasted_iota(jnp.int32, (rows, rows), 1)
            row_id = lax.broadcasted_iota(jnp.int32, (rows, HG_KEY_DIM), 0)
            for h, sl in enumerate(head_slices):

                def column(s, acc, q_h=q[:, sl], k_h=k[:, sl], bc_h=bc[:, sl]):
                    k_s = jnp.sum(jnp.where(row_id == s, k_h, 0.0), axis=0, keepdims=True)
                    bc_s = jnp.sum(jnp.where(row_id == s, bc_h, 0.0), axis=0, keepdims=True)
                    w = q_h * k_s * jnp.exp(jnp.minimum(bc_h - bc_s, 0.0))
                    return jnp.where(col_id == s, jnp.sum(w, axis=-1, keepdims=True), acc)

                attn_ref[h] = lax.fori_loop(0, rows, column, jnp.zeros((rows, rows), F32))

        vb = stack(hi_ref)
        q_state = (q * jnp.exp(bc)).astype(BF16)
        k_state = (k * jnp.exp(b_last - bc)).astype(BF16)
        outs = []
        for h, sl in enumerate(head_slices):
            o = _dot(jnp.where(causal, attn_ref[h], 0.0).astype(BF16), vb[:, sl])
            o_inter = []
            for b in range(batch):
                rs = slice(b * t, (b + 1) * t)
                state = state_ref[b, h]
                o_inter.append(_dot_nt(q_state[rs, sl], state.astype(BF16)))
                state_ref[b, h] = (state * jnp.exp(last_rows[b][:, sl])
                                   + _dot_tn(vb[rs, sl], k_state[rs, sl]))
            o = o + jnp.concatenate(o_inter, axis=0)
            ms = jnp.mean(o * o, axis=-1, keepdims=True)
            outs.append(o * lax.rsqrt(ms + RMS_EPS) * nw_ref[...])
        y = jnp.concatenate(outs, axis=-1) * _silu(stack(hg_ref))
        for b in range(batch):
            o_ref[b, c * t:(c + 1) * t, :] = y[b * t:(b + 1) * t, :].astype(o_ref.dtype)


def _hgrn2_mixer(hq, hf, hi, hg, hg_lower, norm_w, layer):
    batch, seq, _ = hq.shape
    spec = _seq_spec(batch, HG_CHUNK, HG_WIDTH)
    return pl.pallas_call(
        functools.partial(_hgrn2_kernel, layer=layer),
        grid=(seq // (CHUNKS_PER_STEP * HG_CHUNK),),
        in_specs=[spec, spec, spec, spec, _const_spec(hg_lower.shape), _const_spec((1, HG_VAL_DIM))],
        out_specs=spec,
        out_shape=jax.ShapeDtypeStruct((batch, seq, HG_WIDTH), BF16),
        scratch_shapes=[pltpu.VMEM((batch, HG_HEADS, HG_VAL_DIM, HG_KEY_DIM), F32),
                        pltpu.VMEM((HG_HEADS, batch * HG_CHUNK, batch * HG_CHUNK), F32)],
        compiler_params=_params(1),
        name="hgrn2_mixer",
    )(hq, hf, hi, hg, hg_lower.astype(F32), norm_w.astype(F32).reshape(1, -1))


def _swa_kernel(sinks_ref, q_ref, kp_ref, kc_ref, vp_ref, vc_ref, o_ref):
    t = SWA_BLOCK
    batch = q_ref.shape[0]
    group = SWA_Q_HEADS // SWA_KV_HEADS
    rows = group * t

    lane_kv = lax.broadcasted_iota(jnp.int32, (2 * t, LANES), 1)
    lo_kv = lane_kv < SWA_HEAD_DIM
    lane_q = lax.broadcasted_iota(jnp.int32, (t, LANES), 1)
    lo_q = lane_q < SWA_HEAD_DIM
    row = lax.broadcasted_iota(jnp.int32, (rows, 2 * t), 0) & (t - 1)
    col = lax.broadcasted_iota(jnp.int32, (rows, 2 * t), 1)
    band = (col > row) & (col <= row + t)
    head_of_row = lax.broadcasted_iota(jnp.int32, (rows, 1), 0) // t

    for c, qrows, b in _chunks_of_step(t, batch):
        q = q_ref[b, qrows, :].astype(F32) * (SWA_HEAD_DIM ** -0.5)
        if c == 0:
            visible = band & ((col >= t) | (pl.program_id(0) > 0))
            kcat = jnp.concatenate([kp_ref[b], kc_ref[b, 0:t, :]], axis=0).astype(F32)
            vcat = jnp.concatenate([vp_ref[b], vc_ref[b, 0:t, :]], axis=0).astype(F32)
        else:
            visible = band
            kcat = kc_ref[b, (c - 1) * t:(c + 1) * t, :].astype(F32)
            vcat = vc_ref[b, (c - 1) * t:(c + 1) * t, :].astype(F32)
        k_swap = pltpu.roll(kcat, SWA_HEAD_DIM, axis=1)
        v_swap = pltpu.roll(vcat, SWA_HEAD_DIM, axis=1)
        outs = []
        for g in range(SWA_KV_HEADS):
            k_dup = (jnp.where(lo_kv, kcat, k_swap) if g == 0 else jnp.where(lo_kv, k_swap, kcat)).astype(BF16)
            v_dup = (jnp.where(lo_kv, vcat, v_swap) if g == 0 else jnp.where(lo_kv, v_swap, vcat)).astype(BF16)
            pieces = []
            sink = jnp.zeros((rows, 1), F32)
            for hh in range(group):
                head = g * group + hh
                q_pair = q[:, (head // 2) * LANES:(head // 2 + 1) * LANES]
                pieces.append(jnp.where(lo_q if head % 2 == 0 else jnp.logical_not(lo_q), q_pair, 0.0))
                sink = jnp.where(head_of_row == hh, sinks_ref[head], sink)
            s = _dot_nt(jnp.concatenate(pieces, axis=0).astype(BF16), k_dup)
            s = jnp.where(visible, s, MASK_VALUE)
            m = jnp.maximum(jnp.max(s, axis=-1, keepdims=True), sink)
            p = jnp.exp(s - m)
            denom = jnp.sum(p, axis=-1, keepdims=True) + jnp.exp(sink - m)
            o = _dot((p / denom).astype(BF16), v_dup)
            for pp in range(group // 2):
                outs.append(jnp.where(lo_q, o[2 * pp * t:(2 * pp + 1) * t, :], o[(2 * pp + 1) * t:(2 * pp + 2) * t, :]))
        o_ref[b, qrows, :] = jnp.concatenate(outs, axis=-1).astype(o_ref.dtype)


def _swa_attention(q, k, v, sinks):
    batch, seq, _ = q.shape
    cur = lambda n: pl.BlockSpec((batch, CHUNKS_PER_STEP * SWA_BLOCK, n), lambda i: (0, i, 0))
    prev = lambda n: pl.BlockSpec((batch, SWA_BLOCK, n), lambda i: (0, jnp.maximum(CHUNKS_PER_STEP * i - 1, 0), 0))
    return pl.pallas_call(
        _swa_kernel,
        grid=(seq // (CHUNKS_PER_STEP * SWA_BLOCK),),
        in_specs=[pl.BlockSpec(memory_space=pltpu.SMEM), cur(SWA_Q_DIM),
                  prev(SWA_KV_DIM), cur(SWA_KV_DIM), prev(SWA_KV_DIM), cur(SWA_KV_DIM)],
        out_specs=cur(SWA_Q_DIM),
        out_shape=jax.ShapeDtypeStruct((batch, seq, SWA_Q_DIM), BF16),
        compiler_params=_params(1),
        name="swa_attention",
    )(sinks.astype(F32), q, k, k, v, v)


def _rglru_kernel(gate_ref, xr_ref, cw_ref, cb_ref, wg_ref, bg_ref, lam_ref, o_ref, h_ref, xcat_ref):
    t = RG_CHUNK
    batch = gate_ref.shape[0]
    groups = t // SUBLANES

    @pl.when(pl.program_id(0) == 0)
    def _():
        h_ref[...] = jnp.zeros_like(h_ref)
        xcat_ref[:, :, 0:SUBLANES, :] = jnp.zeros((batch, RG_WIDTH // LANES, SUBLANES, LANES), F32)

    cw = cw_ref[...]
    cb = cb_ref[...]
    decay_rate = (-RG_C) * _softplus(-lam_ref[...])
    sub = lax.broadcasted_iota(jnp.int32, (groups, SUBLANES, LANES), 1)
    for _, rows, b in _chunks_of_step(t, batch):
        xc = _causal_conv(xr_ref[b, rows, :], xcat_ref, b, cw, cb, RG_CONV, t)
        gates = _dot(xc.astype(BF16), wg_ref[...]) + bg_ref[...]
        log_a = decay_rate * _sigmoid(gates[:, :RG_WIDTH])
        a = jnp.exp(log_a)
        th = jnp.tanh(log_a)
        u = jnp.sqrt(jnp.maximum(-2.0 * th / (1.0 - th), 0.0)) * (_sigmoid(gates[:, RG_WIDTH:]) * xc)
        outs = []
        for c in range(RG_WIDTH // LANES):
            sl = slice(c * LANES, (c + 1) * LANES)
            a_c = a[:, sl].reshape(groups, SUBLANES, LANES)
            u_c = u[:, sl].reshape(groups, SUBLANES, LANES)
            d = 1
            while d < SUBLANES:
                keep = sub >= d
                u_c = u_c + a_c * jnp.where(keep, pltpu.roll(u_c, d, axis=1), 0.0)
                a_c = a_c * jnp.where(keep, pltpu.roll(a_c, d, axis=1), 1.0)
                d *= 2
            h_prev = h_ref[b, 0:1, sl]
            h_groups = []
            for g in range(groups):
                h_g = u_c[g] + a_c[g] * h_prev
                h_prev = h_g[SUBLANES - 1:SUBLANES, :]
                h_groups.append(h_g)
            h_ref[b, 0:1, sl] = h_prev
            outs.append(jnp.concatenate(h_groups, axis=0))
        o_ref[b, rows, :] = (jnp.concatenate(outs, axis=-1) * _gelu_tanh(gate_ref[b, rows, :])).astype(o_ref.dtype)


def _block_diag(w):
    nb, d, _ = w.shape
    eye = jnp.eye(nb, dtype=w.dtype)
    return (eye[:, None, :, None] * w[:, :, None, :]).reshape(nb * d, nb * d)


def _rglru_mixer(gate, xr, conv_w, conv_b, wa, ba, wx, bx, lam):
    batch, seq, _ = gate.shape
    spec = _seq_spec(batch, RG_CHUNK, RG_WIDTH)
    w_gates = jnp.concatenate([_block_diag(wa.astype(F32)), _block_diag(wx.astype(F32))], axis=1).astype(BF16)
    b_gates = jnp.concatenate([ba, bx]).astype(F32).reshape(1, -1)
    return pl.pallas_call(
        _rglru_kernel,
        grid=(seq // (CHUNKS_PER_STEP * RG_CHUNK),),
        in_specs=[spec, spec, _const_spec((RG_CONV, RG_WIDTH)), _const_spec((1, RG_WIDTH)),
                  _const_spec((RG_WIDTH, 2 * RG_WIDTH)), _const_spec((1, 2 * RG_WIDTH)),
                  _const_spec((1, RG_WIDTH))],
        out_specs=spec,
        out_shape=jax.ShapeDtypeStruct((batch, seq, RG_WIDTH), BF16),
        scratch_shapes=[pltpu.VMEM((batch, SUBLANES, RG_WIDTH), F32),
                        pltpu.VMEM((batch, RG_WIDTH // LANES, RG_CHUNK + SUBLANES, LANES), F32)],
        compiler_params=_params(1),
        name="rglru_mixer",
    )(gate, xr, conv_w.astype(F32), conv_b.astype(F32).reshape(1, -1), w_gates, b_gates,
      lam.astype(F32).reshape(1, -1))


def _ffn_up(xb, wup_ref, cw_ref, cb_ref, carry_ref, h_ref, act_ref):
    tm = ROW_TILE
    halo = SUBLANES
    fc = FFN_COL_CHUNK
    for c in range(FFN_DIM // fc):
        h = [_dot(xb, wup_ref[:, part * FFN_DIM + c * fc:part * FFN_DIM + (c + 1) * fc]) for part in range(2)]
        for s in range(fc // LANES):
            conv = []
            for part in range(2):
                col = part * FFN_DIM + c * fc + s * LANES
                hb = h_ref.at[c % 2, part, s]
                hb[0:halo, :] = carry_ref[:, col:col + LANES]
                hb[halo:halo + tm, :] = h[part][:, s * LANES:(s + 1) * LANES]
                carry_ref[:, col:col + LANES] = hb[tm:tm + halo, :]
                y = cb_ref[:, col:col + LANES] + cw_ref[2:3, col:col + LANES] * hb[halo:halo + tm, :]
                for k in range(1, FFN_CONV):
                    y = y + cw_ref[2 - k:3 - k, col:col + LANES] * hb[halo - k:halo - k + tm, :]
                conv.append(y)
            act_col = c * fc + s * LANES
            act_ref[:, act_col:act_col + LANES] = (_silu(conv[0]) * conv[1]).astype(BF16)


def _tail_kernel(ya0_ref, yb0_ref, x0_ref, yan_ref, ybn_ref, xn_ref, wout_ref, lng_ref, lnb_ref, wup_ref, cw_ref,
                 cb_ref, wdown_ref, o_ref, carry_ref, h_ref, act_ref, x1_ref, *, tiles_per_seq):
    j = pl.program_id(0)

    def head(ya_ref, yb_ref, x_ref):
        half = ya_ref.shape[1]
        m = _dot(ya_ref[...], wout_ref[0:half, :]) + _dot(yb_ref[...], wout_ref[half:2 * half, :])
        return _layer_norm(ALPHA * x_ref[...] + m, lng_ref[0:1, :], lnb_ref[0:1, :])

    @pl.when(j == 0)
    def _():
        x1_ref[0] = head(ya0_ref, yb0_ref, x0_ref)

    @pl.when(lax.rem(j, tiles_per_seq) == 0)
    def _():
        carry_ref[...] = jnp.zeros_like(carry_ref)

    def step(cur, nxt):
        _ffn_up(x1_ref[cur].astype(BF16), wup_ref, cw_ref, cb_ref, carry_ref, h_ref, act_ref)
        x1_ref[nxt] = head(yan_ref, ybn_ref, xn_ref)
        f = _dot(act_ref[...], wdown_ref[...])
        o_ref[...] = _layer_norm(ALPHA * x1_ref[cur] + f, lng_ref[1:2, :], lnb_ref[1:2, :])

    @pl.when(lax.rem(j, 2) == 0)
    def _():
        step(0, 1)

    @pl.when(lax.rem(j, 2) == 1)
    def _():
        step(1, 0)


def _layer_tail(ya, yb, x, w_out_all, j, ln_g_all, ln_b_all, w_up_all, conv_w_all, conv_b_all, w_down_all, layer, seq):
    rows, d = x.shape
    half = ya.shape[1]
    ntile = rows // ROW_TILE
    first = lambda n: pl.BlockSpec((ROW_TILE, n), lambda t: (0, 0), pipeline_mode=pl.Buffered(1))
    nxt = lambda n: pl.BlockSpec((ROW_TILE, n), lambda t: (jnp.minimum(t + 1, ntile - 1), 0))
    return pl.pallas_call(
        functools.partial(_tail_kernel, tiles_per_seq=seq // ROW_TILE),
        grid=(ntile,),
        in_specs=[first(half), first(half), first(d), nxt(half), nxt(half), nxt(d),
                  _layer_spec((2 * half, d), j), _layer_spec((2, d), layer), _layer_spec((2, d), layer),
                  _layer_spec((d, 2 * FFN_DIM), layer), _layer_spec((FFN_CONV, 2 * FFN_DIM), layer),
                  _layer_spec((1, 2 * FFN_DIM), layer), _layer_spec((FFN_DIM, d), layer)],
        out_specs=pl.BlockSpec((ROW_TILE, d), lambda t: (t, 0)),
        out_shape=jax.ShapeDtypeStruct((rows, d), F32),
        scratch_shapes=[pltpu.VMEM((SUBLANES, 2 * FFN_DIM), F32),
                        pltpu.VMEM((2, 2, FFN_COL_CHUNK // LANES, ROW_TILE + SUBLANES, LANES), F32),
                        pltpu.VMEM((ROW_TILE, FFN_DIM), BF16),
                        pltpu.VMEM((2, ROW_TILE, d), F32)],
        compiler_params=_params(1),
        name="layer_tail",
    )(ya, yb, x, ya, yb, x, w_out_all, ln_g_all, ln_b_all, w_up_all, conv_w_all, conv_b_all, w_down_all)


AB_SPLITS = (SSD_D_INNER, SSD_CONV_DIM, LANES, HG_WIDTH, HG_WIDTH, HG_WIDTH, HG_WIDTH)
CD_SPLITS = (SWA_Q_DIM, SWA_KV_DIM, SWA_KV_DIM, RG_WIDTH, RG_WIDTH)
AB_DTYPES = (F32, F32, F32, F32, F32, BF16, F32)
CD_DTYPES = (BF16, BF16, BF16, F32, F32)


def _ab_weight(w_in):
    a = SSD_D_INNER + SSD_CONV_DIM
    dt_cols = jnp.pad(w_in[..., a:a + SSD_HEADS], ((0, 0), (0, 0), (0, LANES - SSD_HEADS)))
    return jnp.concatenate([w_in[..., :a], dt_cols, w_in[..., a + SSD_HEADS:]], axis=-1).astype(BF16)


def kernel(x, ab_w_in, ssd_conv_w, ssd_conv_b, ssd_dt_bias, ssd_a_log, ssd_d, ssd_norm_w, hg_lower, hg_norm_w, ab_w_out, cd_w_in, swa_sinks, rg_conv_w, rg_conv_b, rg_wa, rg_ba, rg_wx, rg_bx, rg_lambda, cd_w_out, ffn_w_up, ffn_conv_w, ffn_conv_b, ffn_w_down, ln_g, ln_b):
    batch, seq, d = x.shape
    assert d == D_MODEL and seq % ROW_TILE == 0
    rows = batch * seq
    seq_view = lambda arrs: [a.reshape(batch, seq, a.shape[-1]) for a in arrs]
    xr = x.reshape(rows, d).astype(F32)
    ab_w_in_b, cd_w_in_b = _ab_weight(ab_w_in), cd_w_in.astype(BF16)
    ab_w_out_b, cd_w_out_b = ab_w_out.astype(BF16), cd_w_out.astype(BF16)
    w_up_b, w_down_b = ffn_w_up.astype(BF16), ffn_w_down.astype(BF16)
    conv_w, conv_b = ffn_conv_w.astype(F32), ffn_conv_b.astype(F32).reshape(DEPTH, 1, 2 * FFN_DIM)
    ln_g, ln_b = ln_g.astype(F32), ln_b.astype(F32)
    for layer in range(DEPTH):
        j = layer // 2
        if layer % 2 == 0:
            z, xbc, dt_raw, hq, hf, hi, hg = seq_view(_inproj(xr, ab_w_in_b, j, AB_SPLITS, AB_DTYPES))
            ya = _ssd_mixer(z, xbc, dt_raw, ssd_conv_w[j], ssd_conv_b[j], ssd_dt_bias[j], ssd_a_log[j], ssd_d[j],
                            ssd_norm_w[j])
            yb = _hgrn2_mixer(hq, hf, hi, hg, hg_lower, hg_norm_w[j], j)
            w_out_b = ab_w_out_b
        else:
            q, k, v, gate, xg = seq_view(_inproj(xr, cd_w_in_b, j, CD_SPLITS, CD_DTYPES))
            ya = _swa_attention(q, k, v, swa_sinks[j])
            yb = _rglru_mixer(gate, xg, rg_conv_w[j], rg_conv_b[j], rg_wa[j], rg_ba[j], rg_wx[j], rg_bx[j],
                              rg_lambda[j])
            w_out_b = cd_w_out_b
        xr = _layer_tail(ya.reshape(rows, -1), yb.reshape(rows, -1), xr, w_out_b, j, ln_g, ln_b,
                         w_up_b, conv_w, conv_b, w_down_b, layer, seq)
    return xr.reshape(batch, seq, d).astype(x.dtype)
```

```python
import functools
import itertools

import jax
import jax.numpy as jnp
import numpy as np
from jax import lax
from jax.experimental import pallas as pl
from jax.experimental.pallas import tpu as pltpu

F32 = jnp.float32
BF16 = jnp.bfloat16

LANES = 128
SUBLANES = 8
VMEM_LIMIT_BYTES = 56 * 1024 * 1024

D_MODEL = 1024
DEPTH = 4
SSD_HEADS = 8
SSD_HEAD_DIM = 64
SSD_D_INNER = SSD_HEADS * SSD_HEAD_DIM
SSD_GROUPS = 2
SSD_D_STATE = 64
SSD_CONV = 4
SSD_CONV_DIM = SSD_D_INNER + 2 * SSD_GROUPS * SSD_D_STATE
SSD_CHUNK = 128
SSD_HEAD_ROWS = 16
HG_HEADS = 4
HG_KEY_DIM = 128
HG_VAL_DIM = 128
HG_WIDTH = HG_HEADS * HG_KEY_DIM
HG_CHUNK = 64
HG_SAFE_DECAY = 160.0
SWA_Q_HEADS = 8
SWA_KV_HEADS = 2
SWA_HEAD_DIM = 64
SWA_BLOCK = 128
SWA_Q_DIM = SWA_Q_HEADS * SWA_HEAD_DIM
SWA_KV_DIM = SWA_KV_HEADS * SWA_HEAD_DIM
RG_WIDTH = 512
RG_BLOCKS = 8
RG_BLOCK_DIM = RG_WIDTH // RG_BLOCKS
RG_CONV = 4
RG_C = 8.0
RG_CHUNK = 128
FFN_DIM = 2816
FFN_CONV = 3
FFN_COL_CHUNK = 256
ROW_TILE = 512
CHUNKS_PER_STEP = 4
INPROJ_ROW_TILE = 1024
LN_EPS = 1e-5
RMS_EPS = 1e-6
MASK_VALUE = -1e9
ALPHA = (2 * DEPTH) ** 0.25


def _params(n_axes):
    return pltpu.CompilerParams(dimension_semantics=("arbitrary",) * n_axes,
                                vmem_limit_bytes=VMEM_LIMIT_BYTES)


def _const_spec(shape):
    return pl.BlockSpec(shape, lambda *_: (0,) * len(shape), pipeline_mode=pl.Buffered(1))


def _layer_spec(shape, layer):
    return pl.BlockSpec((None,) + tuple(shape), lambda *_: (layer,) + (0,) * len(shape),
                        pipeline_mode=pl.Buffered(1))


def _sigmoid(x):
    return 1.0 / (1.0 + jnp.exp(-x))


def _silu(x):
    half = 0.5 * x
    return half * jnp.tanh(half) + half


def _softplus(x):
    return jnp.maximum(x, 0.0) + jnp.log1p(jnp.exp(-jnp.abs(x)))


def _gelu_tanh(x):
    return 0.5 * x * (1.0 + jnp.tanh(np.sqrt(2.0 / np.pi).astype(np.float32) * (x + 0.044715 * (x * x * x))))


def _layer_norm(r, g, b):
    mu = jnp.mean(r, axis=-1, keepdims=True)
    d = r - mu
    var = jnp.mean(d * d, axis=-1, keepdims=True)
    return d * lax.rsqrt(var + LN_EPS) * g + b


def _dot(a, b):
    return jnp.dot(a, b, preferred_element_type=F32)


def _split3(x):
    x1 = x.astype(BF16)
    r = x - x1.astype(F32)
    x2 = r.astype(BF16)
    return x1, x2, (r - x2.astype(F32)).astype(BF16)


def _dot_nt(a, b):
    return lax.dot_general(a, b, (((1,), (1,)), ((), ())), preferred_element_type=F32)


def _dot_tn(a, b):
    return lax.dot_general(a, b, (((0,), (0,)), ((), ())), preferred_element_type=F32)


def _lower_tri(n):
    row = lax.broadcasted_iota(jnp.int32, (n, n), 0)
    col = lax.broadcasted_iota(jnp.int32, (n, n), 1)
    return col <= row


def _inproj_kernel(x_ref, w_ref, *o_refs, splits):
    xb = x_ref[...].astype(BF16)
    off = 0
    for o_ref, n in zip(o_refs, splits):
        o_ref[...] = _dot(xb, w_ref[:, off:off + n]).astype(o_ref.dtype)
        off += n


def _inproj(x, w_all, layer, splits, dtypes):
    rows, d = x.shape
    n_total = sum(splits)
    return pl.pallas_call(
        functools.partial(_inproj_kernel, splits=splits),
        grid=(rows // INPROJ_ROW_TILE,),
        in_specs=[pl.BlockSpec((INPROJ_ROW_TILE, d), lambda i: (i, 0)), _layer_spec((d, n_total), layer)],
        out_specs=[pl.BlockSpec((INPROJ_ROW_TILE, n), lambda i: (i, 0)) for n in splits],
        out_shape=[jax.ShapeDtypeStruct((rows, n), dt) for n, dt in zip(splits, dtypes)],
        compiler_params=_params(1),
        name="inproj",
    )(x, w_all)


def _seq_spec(batch, t, n):
    return pl.BlockSpec((batch, CHUNKS_PER_STEP * t, n), lambda c: (0, c, 0))


def _chunks_of_step(t, batch):
    return [(c, slice(c * t, (c + 1) * t), b) for c, b in itertools.product(range(CHUNKS_PER_STEP), range(batch))]


def _causal_conv(x_new, xcat_ref, b, cw, cb, width, t):
    halo = SUBLANES
    ys = []
    for s in range(x_new.shape[1] // LANES):
        sl = slice(s * LANES, (s + 1) * LANES)
        xcat_ref[b, s, halo:halo + t, :] = x_new[:, sl]
        y = cb[:, sl] + cw[width - 1:width, sl] * x_new[:, sl]
        for k in range(1, width):
            y = y + cw[width - 1 - k:width - k, sl] * xcat_ref[b, s, halo - k:halo - k + t, :]
        xcat_ref[b, s, 0:halo, :] = xcat_ref[b, s, t:t + halo, :]
        ys.append(y)
    return jnp.concatenate(ys, axis=-1)


def _ssd_kernel(z_ref, xbc_ref, dt_ref, cw_ref, cb_ref, dtb_ref, alog_ref, dskip_ref, nw_ref, expand_ref,
                o_ref, state_ref, xcat_ref):
    t = SSD_CHUNK
    batch = z_ref.shape[0]

    @pl.when(pl.program_id(0) == 0)
    def _():
        state_ref[...] = jnp.zeros_like(state_ref)
        xcat_ref[:, :, 0:SUBLANES, :] = jnp.zeros((batch, SSD_CONV_DIM // LANES, SUBLANES, LANES), F32)

    cw = cw_ref[...]
    cb = cb_ref[...]
    neg_a = -jnp.exp(alog_ref[...])
    expand = expand_ref[...]
    tri = _lower_tri(t)
    upper = lax.broadcasted_iota(jnp.int32, (t, t), 0) <= lax.broadcasted_iota(jnp.int32, (t, t), 1)
    triu3 = jnp.concatenate([jnp.where(upper, 1.0, 0.0).astype(BF16)] * 3, axis=0)
    pad_rows = jnp.zeros((LANES - SSD_HEAD_ROWS, t), F32)
    lane = lax.broadcasted_iota(jnp.int32, (t, LANES), 1)
    lo_half = lane < SSD_D_STATE
    sub = lax.broadcasted_iota(jnp.int32, (LANES, LANES), 0)
    heads_per_group = SSD_HEADS // SSD_GROUPS
    gw = SSD_D_INNER // SSD_GROUPS

    for _, rows, b in _chunks_of_step(t, batch):
        xa = _silu(_causal_conv(xbc_ref[b, rows, :], xcat_ref, b, cw, cb, SSD_CONV, t))
        xs = xa[:, :SSD_D_INNER]
        bm = xa[:, SSD_D_INNER:SSD_D_INNER + LANES]
        cm = xa[:, SSD_D_INNER + LANES:SSD_D_INNER + 2 * LANES]

        dt_t = _softplus((dt_ref[b, rows, :] + dtb_ref[...]).T[:SSD_HEAD_ROWS, :])
        cs_row = _dot(jnp.concatenate(_split3(dt_t * neg_a), axis=1), triu3)
        cs = jnp.concatenate([cs_row, pad_rows], axis=0).T
        per_channel = _dot_tn(jnp.concatenate(_split3(jnp.concatenate([dt_t, cs_row], axis=1)), axis=0), expand)
        dt_f = per_channel[:t, :]
        cs_f = per_channel[t:, :]
        cs_last = cs_f[t - 1:t, :]
        xdt = xs * dt_f
        ecs = jnp.exp(cs_f)
        w_state = (jnp.exp(cs_last - cs_f) * xdt).astype(BF16)
        dec_last = jnp.exp(cs_last)

        bm_t = bm.T.astype(BF16)
        cm_g = [jnp.where(lo_half, cm, 0.0).astype(BF16), jnp.where(lo_half, 0.0, cm).astype(BF16)]
        scores_g = [_dot(c, bm_t) for c in cm_g]

        y_pairs = []
        for pair in range(SSD_HEADS // 2):
            g = (2 * pair) // heads_per_group
            sl = slice(LANES * pair, LANES * (pair + 1))
            x_pair = xdt[:, sl]
            y_pair = None
            for j in range(2):
                h = 2 * pair + j
                x_h = jnp.where(lo_half if j == 0 else jnp.logical_not(lo_half), x_pair, 0.0).astype(BF16)
                diff = cs[:, h:h + 1] - cs_row[h:h + 1, :]
                decay = jnp.where(tri, jnp.exp(jnp.where(tri, diff, 0.0)), 0.0)
                y_h = _dot((scores_g[g] * decay).astype(BF16), x_h)
                y_pair = y_h if y_pair is None else y_pair + y_h
            state = state_ref[b, pair]
            y_pair = y_pair + _dot(cm_g[g], state.astype(BF16)) * ecs[:, sl]
            upd = _dot(bm_t, w_state[:, sl])
            in_group = (sub >= g * SSD_D_STATE) & (sub < (g + 1) * SSD_D_STATE)
            state_ref[b, pair] = jnp.where(in_group, upd, 0.0) + dec_last[:, sl] * state
            y_pairs.append(y_pair)

        y = jnp.concatenate(y_pairs, axis=-1) + dskip_ref[...] * xs
        y = y * _silu(z_ref[b, rows, :])
        outs = []
        for g in range(SSD_GROUPS):
            yg = y[:, g * gw:(g + 1) * gw]
            ms = jnp.mean(yg * yg, axis=-1, keepdims=True)
            outs.append(yg * lax.rsqrt(ms + RMS_EPS) * nw_ref[:, g * gw:(g + 1) * gw])
        o_ref[b, rows, :] = jnp.concatenate(outs, axis=-1).astype(o_ref.dtype)


def _ssd_mixer(z, xbc, dt_raw, conv_w, conv_b, dt_bias, a_log, d_skip, norm_w):
    batch, seq, _ = z.shape
    dt_bias_lanes = jnp.pad(dt_bias.astype(F32), (0, LANES - SSD_HEADS)).reshape(1, LANES)
    a_log_rows = jnp.broadcast_to(jnp.pad(a_log.astype(F32), (0, SSD_HEAD_ROWS - SSD_HEADS))[:, None],
                                  (SSD_HEAD_ROWS, SSD_CHUNK))
    expand = (jnp.arange(SSD_HEAD_ROWS)[:, None] == (jnp.arange(SSD_D_INNER) // SSD_HEAD_DIM)[None, :]).astype(BF16)
    expand = jnp.concatenate([expand] * 3, axis=0)
    return pl.pallas_call(
        _ssd_kernel,
        grid=(seq // (CHUNKS_PER_STEP * SSD_CHUNK),),
        in_specs=[_seq_spec(batch, SSD_CHUNK, SSD_D_INNER), _seq_spec(batch, SSD_CHUNK, SSD_CONV_DIM),
                  _seq_spec(batch, SSD_CHUNK, LANES),
                  _const_spec((SSD_CONV, SSD_CONV_DIM)), _const_spec((1, SSD_CONV_DIM)),
                  _const_spec((1, LANES)), _const_spec((SSD_HEAD_ROWS, SSD_CHUNK)),
                  _const_spec((1, SSD_D_INNER)), _const_spec((1, SSD_D_INNER)),
                  _const_spec((3 * SSD_HEAD_ROWS, SSD_D_INNER))],
        out_specs=_seq_spec(batch, SSD_CHUNK, SSD_D_INNER),
        out_shape=jax.ShapeDtypeStruct((batch, seq, SSD_D_INNER), BF16),
        scratch_shapes=[pltpu.VMEM((batch, SSD_HEADS // 2, LANES, LANES), F32),
                        pltpu.VMEM((batch, SSD_CONV_DIM // LANES, SSD_CHUNK + SUBLANES, LANES), F32)],
        compiler_params=_params(1),
        name="ssd_mixer",
    )(z, xbc, dt_raw, conv_w.astype(F32), conv_b.astype(F32).reshape(1, -1), dt_bias_lanes, a_log_rows,
      jnp.repeat(d_skip.astype(F32), SSD_HEAD_DIM).reshape(1, -1), norm_w.astype(F32).reshape(1, -1), expand)


def _hgrn2_kernel(hq_ref, hf_ref, hi_ref, hg_ref, lower_ref, nw_ref, o_ref, state_ref, attn_ref, *, layer):
    t = HG_CHUNK
    batch = hq_ref.shape[0]
    rows = batch * t

    @pl.when(pl.program_id(0) == 0)
    def _():
        state_ref[...] = jnp.zeros_like(state_ref)

    low = lower_ref[...]
    ex = jnp.exp(low - jnp.max(low, axis=0, keepdims=True))
    sm = ex / jnp.sum(ex, axis=0, keepdims=True)
    lb = jnp.clip(jnp.sum(sm[:layer + 1, :], axis=0, keepdims=True) - sm[0:1, :], 0.0, 1.0)

    row = lax.broadcasted_iota(jnp.int32, (rows, rows), 0)
    col = lax.broadcasted_iota(jnp.int32, (rows, rows), 1)
    causal = (col <= row) & ((row & -t) == (col & -t))
    causal_b = jnp.where(causal, 1.0, 0.0).astype(BF16)
    head_slices = [slice(h * HG_KEY_DIM, (h + 1) * HG_KEY_DIM) for h in range(HG_HEADS)]

    for c in range(CHUNKS_PER_STEP):
        stack = lambda ref, c=c: jnp.concatenate([ref[b, c * t:(c + 1) * t, :] for b in range(batch)], axis=0)
        q = _silu(stack(hq_ref))
        fx = stack(hf_ref)
        e = jnp.exp(-jnp.abs(fx))
        big = 1.0 / (1.0 + e)
        small = e * big
        pos = fx >= 0.0
        log_f = jnp.log(lb + (1.0 - lb) * jnp.where(pos, big, small))
        k = (1.0 - lb) * jnp.where(pos, small, big)
        bc = _dot(jnp.concatenate([causal_b] * 3, axis=1), jnp.concatenate(_split3(log_f), axis=0))
        last_rows = [bc[(b + 1) * t - 1:(b + 1) * t, :] for b in range(batch)]
        b_last = jnp.concatenate([jnp.broadcast_to(r, (t, HG_WIDTH)) for r in last_rows], axis=0)

        mid = 0.5 * b_last
        q_mid = (q * jnp.exp(bc - mid)).astype(BF16)
        k_mid = (k * jnp.exp(mid - bc)).astype(BF16)
        for h, sl in enumerate(head_slices):
            attn_ref[h] = _dot_nt(q_mid[:, sl], k_mid[:, sl])

        @pl.when(jnp.logical_not(jnp.max(-jnp.concatenate(last_rows, axis=0)) <= HG_SAFE_DECAY))
        def _():
            col_id = lax.broadcasted_iota(jnp.int32, (rows, rows), 1)
            row_id = lax.broadcasted_iota(jnp.int32, (rows, HG_KEY_DIM), 0)
            for h, sl in enumerate(head_slices):

                def column(s, acc, q_h=q[:, sl], k_h=k[:, sl], bc_h=bc[:, sl]):
                    k_s = jnp.sum(jnp.where(row_id == s, k_h, 0.0), axis=0, keepdims=True)
                    bc_s = jnp.sum(jnp.where(row_id == s, bc_h, 0.0), axis=0, keepdims=True)
                    w = q_h * k_s * jnp.exp(jnp.minimum(bc_h - bc_s, 0.0))
                    return jnp.where(col_id == s, jnp.sum(w, axis=-1, keepdims=True), acc)

                attn_ref[h] = lax.fori_loop(0, rows, column, jnp.zeros((rows, rows), F32))

        vb = stack(hi_ref)
        q_state = (q * jnp.exp(bc)).astype(BF16)
        k_state = (k * jnp.exp(b_last - bc)).astype(BF16)
        outs = []
        for h, sl in enumerate(head_slices):
            o = _dot(jnp.where(causal, attn_ref[h], 0.0).astype(BF16), vb[:, sl])
            o_inter = []
            for b in range(batch):
                rs = slice(b * t, (b + 1) * t)
                state = state_ref[b, h]
                o_inter.append(_dot_nt(q_state[rs, sl], state.astype(BF16)))
                state_ref[b, h] = (state * jnp.exp(last_rows[b][:, sl])
                                   + _dot_tn(vb[rs, sl], k_state[rs, sl]))
            o = o + jnp.concatenate(o_inter, axis=0)
            ms = jnp.mean(o * o, axis=-1, keepdims=True)
            outs.append(o * lax.rsqrt(ms + RMS_EPS) * nw_ref[...])
        y = jnp.concatenate(outs, axis=-1) * _silu(stack(hg_ref))
        for b in range(batch):
            o_ref[b, c * t:(c + 1) * t, :] = y[b * t:(b + 1) * t, :].astype(o_ref.dtype)


def _hgrn2_mixer(hq, hf, hi, hg, hg_lower, norm_w, layer):
    batch, seq, _ = hq.shape
    spec = _seq_spec(batch, HG_CHUNK, HG_WIDTH)
    return pl.pallas_call(
        functools.partial(_hgrn2_kernel, layer=layer),
        grid=(seq // (CHUNKS_PER_STEP * HG_CHUNK),),
        in_specs=[spec, spec, spec, spec, _const_spec(hg_lower.shape), _const_spec((1, HG_VAL_DIM))],
        out_specs=spec,
        out_shape=jax.ShapeDtypeStruct((batch, seq, HG_WIDTH), BF16),
        scratch_shapes=[pltpu.VMEM((batch, HG_HEADS, HG_VAL_DIM, HG_KEY_DIM), F32),
                        pltpu.VMEM((HG_HEADS, batch * HG_CHUNK, batch * HG_CHUNK), F32)],
        compiler_params=_params(1),
        name="hgrn2_mixer",
    )(hq, hf, hi, hg, hg_lower.astype(F32), norm_w.astype(F32).reshape(1, -1))


def _swa_kernel(sinks_ref, q_ref, kp_ref, kc_ref, vp_ref, vc_ref, o_ref):
    t = SWA_BLOCK
    batch = q_ref.shape[0]
    group = SWA_Q_HEADS // SWA_KV_HEADS
    rows = group * t

    lane_kv = lax.broadcasted_iota(jnp.int32, (2 * t, LANES), 1)
    lo_kv = lane_kv < SWA_HEAD_DIM
    lane_q = lax.broadcasted_iota(jnp.int32, (t, LANES), 1)
    lo_q = lane_q < SWA_HEAD_DIM
    row = lax.broadcasted_iota(jnp.int32, (rows, 2 * t), 0) & (t - 1)
    col = lax.broadcasted_iota(jnp.int32, (rows, 2 * t), 1)
    band = (col > row) & (col <= row + t)
    head_of_row = lax.broadcasted_iota(jnp.int32, (rows, 1), 0) // t

    for c, qrows, b in _chunks_of_step(t, batch):
        q = q_ref[b, qrows, :].astype(F32) * (SWA_HEAD_DIM ** -0.5)
        if c == 0:
            visible = band & ((col >= t) | (pl.program_id(0) > 0))
            kcat = jnp.concatenate([kp_ref[b], kc_ref[b, 0:t, :]], axis=0).astype(F32)
            vcat = jnp.concatenate([vp_ref[b], vc_ref[b, 0:t, :]], axis=0).astype(F32)
        else:
            visible = band
            kcat = kc_ref[b, (c - 1) * t:(c + 1) * t, :].astype(F32)
            vcat = vc_ref[b, (c - 1) * t:(c + 1) * t, :].astype(F32)
        k_swap = pltpu.roll(kcat, SWA_HEAD_DIM, axis=1)
        v_swap = pltpu.roll(vcat, SWA_HEAD_DIM, axis=1)
        outs = []
        for g in range(SWA_KV_HEADS):
            k_dup = (jnp.where(lo_kv, kcat, k_swap) if g == 0 else jnp.where(lo_kv, k_swap, kcat)).astype(BF16)
            v_dup = (jnp.where(lo_kv, vcat, v_swap) if g == 0 else jnp.where(lo_kv, v_swap, vcat)).astype(BF16)
            pieces = []
            sink = jnp.zeros((rows, 1), F32)
            for hh in range(group):
                head = g * group + hh
                q_pair = q[:, (head // 2) * LANES:(head // 2 + 1) * LANES]
                pieces.append(jnp.where(lo_q if head % 2 == 0 else jnp.logical_not(lo_q), q_pair, 0.0))
                sink = jnp.where(head_of_row == hh, sinks_ref[head], sink)
            s = _dot_nt(jnp.concatenate(pieces, axis=0).astype(BF16), k_dup)
            s = jnp.where(visible, s, MASK_VALUE)
            m = jnp.maximum(jnp.max(s, axis=-1, keepdims=True), sink)
            p = jnp.exp(s - m)
            denom = jnp.sum(p, axis=-1, keepdims=True) + jnp.exp(sink - m)
            o = _dot((p / denom).astype(BF16), v_dup)
            for pp in range(group // 2):
                outs.append(jnp.where(lo_q, o[2 * pp * t:(2 * pp + 1) * t, :], o[(2 * pp + 1) * t:(2 * pp + 2) * t, :]))
        o_ref[b, qrows, :] = jnp.concatenate(outs, axis=-1).astype(o_ref.dtype)


def _swa_attention(q, k, v, sinks):
    batch, seq, _ = q.shape
    cur = lambda n: pl.BlockSpec((batch, CHUNKS_PER_STEP * SWA_BLOCK, n), lambda i: (0, i, 0))
    prev = lambda n: pl.BlockSpec((batch, SWA_BLOCK, n), lambda i: (0, jnp.maximum(CHUNKS_PER_STEP * i - 1, 0), 0))
    return pl.pallas_call(
        _swa_kernel,
        grid=(seq // (CHUNKS_PER_STEP * SWA_BLOCK),),
        in_specs=[pl.BlockSpec(memory_space=pltpu.SMEM), cur(SWA_Q_DIM),
                  prev(SWA_KV_DIM), cur(SWA_KV_DIM), prev(SWA_KV_DIM), cur(SWA_KV_DIM)],
        out_specs=cur(SWA_Q_DIM),
        out_shape=jax.ShapeDtypeStruct((batch, seq, SWA_Q_DIM), BF16),
        compiler_params=_params(1),
        name="swa_attention",
    )(sinks.astype(F32), q, k, k, v, v)


def _rglru_kernel(gate_ref, xr_ref, cw_ref, cb_ref, wg_ref, bg_ref, lam_ref, o_ref, h_ref, xcat_ref):
    t = RG_CHUNK
    batch = gate_ref.shape[0]
    groups = t // SUBLANES

    @pl.when(pl.program_id(0) == 0)
    def _():
        h_ref[...] = jnp.zeros_like(h_ref)
        xcat_ref[:, :, 0:SUBLANES, :] = jnp.zeros((batch, RG_WIDTH // LANES, SUBLANES, LANES), F32)

    cw = cw_ref[...]
    cb = cb_ref[...]
    decay_rate = (-RG_C) * _softplus(-lam_ref[...])
    sub = lax.broadcasted_iota(jnp.int32, (groups, SUBLANES, LANES), 1)
    for _, rows, b in _chunks_of_step(t, batch):
        xc = _causal_conv(xr_ref[b, rows, :], xcat_ref, b, cw, cb, RG_CONV, t)
        gates = _dot(xc.astype(BF16), wg_ref[...]) + bg_ref[...]
        log_a = decay_rate * _sigmoid(gates[:, :RG_WIDTH])
        a = jnp.exp(log_a)
        th = jnp.tanh(log_a)
        u = jnp.sqrt(jnp.maximum(-2.0 * th / (1.0 - th), 0.0)) * (_sigmoid(gates[:, RG_WIDTH:]) * xc)
        outs = []
        for c in range(RG_WIDTH // LANES):
            sl = slice(c * LANES, (c + 1) * LANES)
            a_c = a[:, sl].reshape(groups, SUBLANES, LANES)
            u_c = u[:, sl].reshape(groups, SUBLANES, LANES)
            d = 1
            while d < SUBLANES:
                keep = sub >= d
                u_c = u_c + a_c * jnp.where(keep, pltpu.roll(u_c, d, axis=1), 0.0)
                a_c = a_c * jnp.where(keep, pltpu.roll(a_c, d, axis=1), 1.0)
                d *= 2
            h_prev = h_ref[b, 0:1, sl]
            h_groups = []
            for g in range(groups):
                h_g = u_c[g] + a_c[g] * h_prev
                h_prev = h_g[SUBLANES - 1:SUBLANES, :]
                h_groups.append(h_g)
            h_ref[b, 0:1, sl] = h_prev
            outs.append(jnp.concatenate(h_groups, axis=0))
        o_ref[b, rows, :] = (jnp.concatenate(outs, axis=-1) * _gelu_tanh(gate_ref[b, rows, :])).astype(o_ref.dtype)


def _block_diag(w):
    nb, d, _ = w.shape
    eye = jnp.eye(nb, dtype=w.dtype)
    return (eye[:, None, :, None] * w[:, :, None, :]).reshape(nb * d, nb * d)


def _rglru_mixer(gate, xr, conv_w, conv_b, wa, ba, wx, bx, lam):
    batch, seq, _ = gate.shape
    spec = _seq_spec(batch, RG_CHUNK, RG_WIDTH)
    w_gates = jnp.concatenate([_block_diag(wa.astype(F32)), _block_diag(wx.astype(F32))], axis=1).astype(BF16)
    b_gates = jnp.concatenate([ba, bx]).astype(F32).reshape(1, -1)
    return pl.pallas_call(
        _rglru_kernel,
        grid=(seq // (CHUNKS_PER_STEP * RG_CHUNK),),
        in_specs=[spec, spec, _const_spec((RG_CONV, RG_WIDTH)), _const_spec((1, RG_WIDTH)),
                  _const_spec((RG_WIDTH, 2 * RG_WIDTH)), _const_spec((1, 2 * RG_WIDTH)),
                  _const_spec((1, RG_WIDTH))],
        out_specs=spec,
        out_shape=jax.ShapeDtypeStruct((batch, seq, RG_WIDTH), BF16),
        scratch_shapes=[pltpu.VMEM((batch, SUBLANES, RG_WIDTH), F32),
                        pltpu.VMEM((batch, RG_WIDTH // LANES, RG_CHUNK + SUBLANES, LANES), F32)],
        compiler_params=_params(1),
        name="rglru_mixer",
    )(gate, xr, conv_w.astype(F32), conv_b.astype(F32).reshape(1, -1), w_gates, b_gates,
      lam.astype(F32).reshape(1, -1))


def _ffn_up(xb, wup_ref, cw_ref, cb_ref, carry_ref, h_ref, act_ref):
    tm = ROW_TILE
    halo = SUBLANES
    fc = FFN_COL_CHUNK
    for c in range(FFN_DIM // fc):
        h = [_dot(xb, wup_ref[:, part * FFN_DIM + c * fc:part * FFN_DIM + (c + 1) * fc]) for part in range(2)]
        for s in range(fc // LANES):
            conv = []
            for part in range(2):
                col = part * FFN_DIM + c * fc + s * LANES
                hb = h_ref.at[c % 2, part, s]
                hb[0:halo, :] = carry_ref[:, col:col + LANES]
                hb[halo:halo + tm, :] = h[part][:, s * LANES:(s + 1) * LANES]
                carry_ref[:, col:col + LANES] = hb[tm:tm + halo, :]
                y = cb_ref[:, col:col + LANES] + cw_ref[2:3, col:col + LANES] * hb[halo:halo + tm, :]
                for k in range(1, FFN_CONV):
                    y = y + cw_ref[2 - k:3 - k, col:col + LANES] * hb[halo - k:halo - k + tm, :]
                conv.append(y)
            act_col = c * fc + s * LANES
            act_ref[:, act_col:act_col + LANES] = (_silu(conv[0]) * conv[1]).astype(BF16)


def _tail_kernel(ya0_ref, yb0_ref, x0_ref, yan_ref, ybn_ref, xn_ref, wout_ref, lng_ref, lnb_ref, wup_ref, cw_ref,
                 cb_ref, wdown_ref, o_ref, carry_ref, h_ref, act_ref, x1_ref, *, tiles_per_seq):
    j = pl.program_id(0)

    def head(ya_ref, yb_ref, x_ref):
        half = ya_ref.shape[1]
        m = _dot(ya_ref[...], wout_ref[0:half, :]) + _dot(yb_ref[...], wout_ref[half:2 * half, :])
        return _layer_norm(ALPHA * x_ref[...] + m, lng_ref[0:1, :], lnb_ref[0:1, :])

    @pl.when(j == 0)
    def _():
        x1_ref[0] = head(ya0_ref, yb0_ref, x0_ref)

    @pl.when(lax.rem(j, tiles_per_seq) == 0)
    def _():
        carry_ref[...] = jnp.zeros_like(carry_ref)

    def step(cur, nxt):
        _ffn_up(x1_ref[cur].astype(BF16), wup_ref, cw_ref, cb_ref, carry_ref, h_ref, act_ref)
        x1_ref[nxt] = head(yan_ref, ybn_ref, xn_ref)
        f = _dot(act_ref[...], wdown_ref[...])
        o_ref[...] = _layer_norm(ALPHA * x1_ref[cur] + f, lng_ref[1:2, :], lnb_ref[1:2, :])

    @pl.when(lax.rem(j, 2) == 0)
    def _():
        step(0, 1)

    @pl.when(lax.rem(j, 2) == 1)
    def _():
        step(1, 0)


def _layer_tail(ya, yb, x, w_out_all, j, ln_g_all, ln_b_all, w_up_all, conv_w_all, conv_b_all, w_down_all, layer, seq):
    rows, d = x.shape
    half = ya.shape[1]
    ntile = rows // ROW_TILE
    first = lambda n: pl.BlockSpec((ROW_TILE, n), lambda t: (0, 0), pipeline_mode=pl.Buffered(1))
    nxt = lambda n: pl.BlockSpec((ROW_TILE, n), lambda t: (jnp.minimum(t + 1, ntile - 1), 0))
    return pl.pallas_call(
        functools.partial(_tail_kernel, tiles_per_seq=seq // ROW_TILE),
        grid=(ntile,),
        in_specs=[first(half), first(half), first(d), nxt(half), nxt(half), nxt(d),
                  _layer_spec((2 * half, d), j), _layer_spec((2, d), layer), _layer_spec((2, d), layer),
                  _layer_spec((d, 2 * FFN_DIM), layer), _layer_spec((FFN_CONV, 2 * FFN_DIM), layer),
                  _layer_spec((1, 2 * FFN_DIM), layer), _layer_spec((FFN_DIM, d), layer)],
        out_specs=pl.BlockSpec((ROW_TILE, d), lambda t: (t, 0)),
        out_shape=jax.ShapeDtypeStruct((rows, d), F32),
        scratch_shapes=[pltpu.VMEM((SUBLANES, 2 * FFN_DIM), F32),
                        pltpu.VMEM((2, 2, FFN_COL_CHUNK // LANES, ROW_TILE + SUBLANES, LANES), F32),
                        pltpu.VMEM((ROW_TILE, FFN_DIM), BF16),
                        pltpu.VMEM((2, ROW_TILE, d), F32)],
        compiler_params=_params(1),
        name="layer_tail",
    )(ya, yb, x, ya, yb, x, w_out_all, ln_g_all, ln_b_all, w_up_all, conv_w_all, conv_b_all, w_down_all)


AB_SPLITS = (SSD_D_INNER, SSD_CONV_DIM, LANES, HG_WIDTH, HG_WIDTH, HG_WIDTH, HG_WIDTH)
CD_SPLITS = (SWA_Q_DIM, SWA_KV_DIM, SWA_KV_DIM, RG_WIDTH, RG_WIDTH)
AB_DTYPES = (F32, F32, F32, F32, F32, BF16, F32)
CD_DTYPES = (BF16, BF16, BF16, F32, F32)


def _ab_weight(w_in):
    a = SSD_D_INNER + SSD_CONV_DIM
    dt_cols = jnp.pad(w_in[..., a:a + SSD_HEADS], ((0, 0), (0, 0), (0, LANES - SSD_HEADS)))
    return jnp.concatenate([w_in[..., :a], dt_cols, w_in[..., a + SSD_HEADS:]], axis=-1).astype(BF16)


def kernel(x, ab_w_in, ssd_conv_w, ssd_conv_b, ssd_dt_bias, ssd_a_log, ssd_d, ssd_norm_w, hg_lower, hg_norm_w, ab_w_out, cd_w_in, swa_sinks, rg_conv_w, rg_conv_b, rg_wa, rg_ba, rg_wx, rg_bx, rg_lambda, cd_w_out, ffn_w_up, ffn_conv_w, ffn_conv_b, ffn_w_down, ln_g, ln_b):
    batch, seq, d = x.shape
    assert d == D_MODEL and seq % ROW_TILE == 0
    rows = batch * seq
    seq_view = lambda arrs: [a.reshape(batch, seq, a.shape[-1]) for a in arrs]
    xr = x.reshape(rows, d).astype(F32)
    ab_w_in_b, cd_w_in_b = _ab_weight(ab_w_in), cd_w_in.astype(BF16)
    ab_w_out_b, cd_w_out_b = ab_w_out.astype(BF16), cd_w_out.astype(BF16)
    w_up_b, w_down_b = ffn_w_up.astype(BF16), ffn_w_down.astype(BF16)
    conv_w, conv_b = ffn_conv_w.astype(F32), ffn_conv_b.astype(F32).reshape(DEPTH, 1, 2 * FFN_DIM)
    ln_g, ln_b = ln_g.astype(F32), ln_b.astype(F32)
    for layer in range(DEPTH):
        j = layer // 2
        if layer % 2 == 0:
            z, xbc, dt_raw, hq, hf, hi, hg = seq_view(_inproj(xr, ab_w_in_b, j, AB_SPLITS, AB_DTYPES))
            ya = _ssd_mixer(z, xbc, dt_raw, ssd_conv_w[j], ssd_conv_b[j], ssd_dt_bias[j], ssd_a_log[j], ssd_d[j],
                            ssd_norm_w[j])
            yb = _hgrn2_mixer(hq, hf, hi, hg, hg_lower, hg_norm_w[j], j)
            w_out_b = ab_w_out_b
        else:
            q, k, v, gate, xg = seq_view(_inproj(xr, cd_w_in_b, j, CD_SPLITS, CD_DTYPES))
            ya = _swa_attention(q, k, v, swa_sinks[j])
            yb = _rglru_mixer(gate, xg, rg_conv_w[j], rg_conv_b[j], rg_wa[j], rg_ba[j], rg_wx[j], rg_bx[j],
                              rg_lambda[j])
            w_out_b = cd_w_out_b
        xr = _layer_tail(ya.reshape(rows, -1), yb.reshape(rows, -1), xr, w_out_b, j, ln_g, ln_b,
                         w_up_b, conv_w, conv_b, w_down_b, layer, seq)
    return xr.reshape(batch, seq, d).astype(x.dtype)
```

```python
import functools
import itertools

import jax
import jax.numpy as jnp
import numpy as np
from jax import lax
from jax.experimental import pallas as pl
from jax.experimental.pallas import tpu as pltpu

F32 = jnp.float32
BF16 = jnp.bfloat16

LANES = 128
SUBLANES = 8
VMEM_LIMIT_BYTES = 56 * 1024 * 1024

D_MODEL = 1024
DEPTH = 4
SSD_HEADS = 8
SSD_HEAD_DIM = 64
SSD_D_INNER = SSD_HEADS * SSD_HEAD_DIM
SSD_GROUPS = 2
SSD_D_STATE = 64
SSD_CONV = 4
SSD_CONV_DIM = SSD_D_INNER + 2 * SSD_GROUPS * SSD_D_STATE
SSD_CHUNK = 128
SSD_HEAD_ROWS = 16
HG_HEADS = 4
HG_KEY_DIM = 128
HG_VAL_DIM = 128
HG_WIDTH = HG_HEADS * HG_KEY_DIM
HG_CHUNK = 64
HG_SAFE_DECAY = 160.0
SWA_Q_HEADS = 8
SWA_KV_HEADS = 2
SWA_HEAD_DIM = 64
SWA_BLOCK = 128
SWA_Q_DIM = SWA_Q_HEADS * SWA_HEAD_DIM
SWA_KV_DIM = SWA_KV_HEADS * SWA_HEAD_DIM
RG_WIDTH = 512
RG_BLOCKS = 8
RG_BLOCK_DIM = RG_WIDTH // RG_BLOCKS
RG_CONV = 4
RG_C = 8.0
RG_CHUNK = 128
FFN_DIM = 2816
FFN_CONV = 3
FFN_COL_CHUNK = 256
ROW_TILE = 512
CHUNKS_PER_STEP = 4
INPROJ_ROW_TILE = 1024
LN_EPS = 1e-5
RMS_EPS = 1e-6
MASK_VALUE = -1e9
ALPHA = (2 * DEPTH) ** 0.25


def _params(n_axes):
    return pltpu.CompilerParams(dimension_semantics=("arbitrary",) * n_axes,
                                vmem_limit_bytes=VMEM_LIMIT_BYTES)


def _const_spec(shape):
    return pl.BlockSpec(shape, lambda *_: (0,) * len(shape), pipeline_mode=pl.Buffered(1))


def _layer_spec(shape, layer):
    return pl.BlockSpec((None,) + tuple(shape), lambda *_: (layer,) + (0,) * len(shape),
                        pipeline_mode=pl.Buffered(1))


def _sigmoid(x):
    return 1.0 / (1.0 + jnp.exp(-x))


def _silu(x):
    half = 0.5 * x
    return half * jnp.tanh(half) + half


def _softplus(x):
    return jnp.maximum(x, 0.0) + jnp.log1p(jnp.exp(-jnp.abs(x)))


def _gelu_tanh(x):
    return 0.5 * x * (1.0 + jnp.tanh(np.sqrt(2.0 / np.pi).astype(np.float32) * (x + 0.044715 * (x * x * x))))


def _layer_norm(r, g, b):
    mu = jnp.mean(r, axis=-1, keepdims=True)
    d = r - mu
    var = jnp.mean(d * d, axis=-1, keepdims=True)
    return d * lax.rsqrt(var + LN_EPS) * g + b


def _dot(a, b):
    return jnp.dot(a, b, preferred_element_type=F32)


def _split3(x):
    x1 = x.astype(BF16)
    r = x - x1.astype(F32)
    x2 = r.astype(BF16)
    return x1, x2, (r - x2.astype(F32)).astype(BF16)


def _dot_nt(a, b):
    return lax.dot_general(a, b, (((1,), (1,)), ((), ())), preferred_element_type=F32)


def _dot_tn(a, b):
    return lax.dot_general(a, b, (((0,), (0,)), ((), ())), preferred_element_type=F32)


def _lower_tri(n):
    row = lax.broadcasted_iota(jnp.int32, (n, n), 0)
    col = lax.broadcasted_iota(jnp.int32, (n, n), 1)
    return col <= row


def _inproj_kernel(x_ref, w_ref, *o_refs, splits):
    xb = x_ref[...].astype(BF16)
    off = 0
    for o_ref, n in zip(o_refs, splits):
        o_ref[...] = _dot(xb, w_ref[:, off:off + n]).astype(o_ref.dtype)
        off += n


def _inproj(x, w_all, layer, splits, dtypes):
    rows, d = x.shape
    n_total = sum(splits)
    return pl.pallas_call(
        functools.partial(_inproj_kernel, splits=splits),
        grid=(rows // INPROJ_ROW_TILE,),
        in_specs=[pl.BlockSpec((INPROJ_ROW_TILE, d), lambda i: (i, 0)), _layer_spec((d, n_total), layer)],
        out_specs=[pl.BlockSpec((INPROJ_ROW_TILE, n), lambda i: (i, 0)) for n in splits],
        out_shape=[jax.ShapeDtypeStruct((rows, n), dt) for n, dt in zip(splits, dtypes)],
        compiler_params=_params(1),
        name="inproj",
    )(x, w_all)


def _seq_spec(batch, t, n):
    return pl.BlockSpec((batch, CHUNKS_PER_STEP * t, n), lambda c: (0, c, 0))


def _chunks_of_step(t, batch):
    return [(c, slice(c * t, (c + 1) * t), b) for c, b in itertools.product(range(CHUNKS_PER_STEP), range(batch))]


def _causal_conv(x_new, xcat_ref, b, cw, cb, width, t):
    halo = SUBLANES
    ys = []
    for s in range(x_new.shape[1] // LANES):
        sl = slice(s * LANES, (s + 1) * LANES)
        xcat_ref[b, s, halo:halo + t, :] = x_new[:, sl]
        y = cb[:, sl] + cw[width - 1:width, sl] * x_new[:, sl]
        for k in range(1, width):
            y = y + cw[width - 1 - k:width - k, sl] * xcat_ref[b, s, halo - k:halo - k + t, :]
        xcat_ref[b, s, 0:halo, :] = xcat_ref[b, s, t:t + halo, :]
        ys.append(y)
    return jnp.concatenate(ys, axis=-1)


def _ssd_kernel(z_ref, xbc_ref, dt_ref, cw_ref, cb_ref, dtb_ref, alog_ref, dskip_ref, nw_ref, expand_ref,
                o_ref, state_ref, xcat_ref):
    t = SSD_CHUNK
    batch = z_ref.shape[0]

    @pl.when(pl.program_id(0) == 0)
    def _():
        state_ref[...] = jnp.zeros_like(state_ref)
        xcat_ref[:, :, 0:SUBLANES, :] = jnp.zeros((batch, SSD_CONV_DIM // LANES, SUBLANES, LANES), F32)

    cw = cw_ref[...]
    cb = cb_ref[...]
    neg_a = -jnp.exp(alog_ref[...])
    expand = expand_ref[...]
    tri = _lower_tri(t)
    upper = lax.broadcasted_iota(jnp.int32, (t, t), 0) <= lax.broadcasted_iota(jnp.int32, (t, t), 1)
    triu3 = jnp.concatenate([jnp.where(upper, 1.0, 0.0).astype(BF16)] * 3, axis=0)
    pad_rows = jnp.zeros((LANES - SSD_HEAD_ROWS, t), F32)
    lane = lax.broadcasted_iota(jnp.int32, (t, LANES), 1)
    lo_half = lane < SSD_D_STATE
    sub = lax.broadcasted_iota(jnp.int32, (LANES, LANES), 0)
    heads_per_group = SSD_HEADS // SSD_GROUPS
    gw = SSD_D_INNER // SSD_GROUPS

    for _, rows, b in _chunks_of_step(t, batch):
        xa = _silu(_causal_conv(xbc_ref[b, rows, :], xcat_ref, b, cw, cb, SSD_CONV, t))
        xs = xa[:, :SSD_D_INNER]
        bm = xa[:, SSD_D_INNER:SSD_D_INNER + LANES]
        cm = xa[:, SSD_D_INNER + LANES:SSD_D_INNER + 2 * LANES]

        dt_t = _softplus((dt_ref[b, rows, :] + dtb_ref[...]).T[:SSD_HEAD_ROWS, :])
        cs_row = _dot(jnp.concatenate(_split3(dt_t * neg_a), axis=1), triu3)
        cs = jnp.concatenate([cs_row, pad_rows], axis=0).T
        head_major = jnp.concatenate(_split3(jnp.concatenate([dt_t, cs_row], axis=1)), axis=0)

        bm_t = bm.T.astype(BF16)
        cm_g = [jnp.where(lo_half, cm, 0.0).astype(BF16), jnp.where(lo_half, 0.0, cm).astype(BF16)]
        scores_g = [_dot(c, bm_t) for c in cm_g]

        y_pairs = []
        for pair in range(SSD_HEADS // 2):
            g = (2 * pair) // heads_per_group
            sl = slice(LANES * pair, LANES * (pair + 1))
            per_channel = _dot_tn(head_major, expand[:, sl])
            dt_f = per_channel[:t, :]
            cs_f = per_channel[t:, :]
            cs_last = cs_f[t - 1:t, :]
            x_pair = xs[:, sl] * dt_f
            w_state = (jnp.exp(cs_last - cs_f) * x_pair).astype(BF16)
            y_pair = None
            for j in range(2):
                h = 2 * pair + j
                x_h = jnp.where(lo_half if j == 0 else jnp.logical_not(lo_half), x_pair, 0.0).astype(BF16)
                diff = cs[:, h:h + 1] - cs_row[h:h + 1, :]
                decay = jnp.where(tri, jnp.exp(jnp.where(tri, diff, 0.0)), 0.0)
                y_h = _dot((scores_g[g] * decay).astype(BF16), x_h)
                y_pair = y_h if y_pair is None else y_pair + y_h
            state = state_ref[b, pair]
            y_pair = y_pair + _dot(cm_g[g], state.astype(BF16)) * jnp.exp(cs_f)
            upd = _dot(bm_t, w_state)
            in_group = (sub >= g * SSD_D_STATE) & (sub < (g + 1) * SSD_D_STATE)
            state_ref[b, pair] = jnp.where(in_group, upd, 0.0) + jnp.exp(cs_last) * state
            y_pair = (y_pair + dskip_ref[:, sl] * xs[:, sl]) * _silu(z_ref[b, rows, sl])
            y_pairs.append(y_pair)

        y = jnp.concatenate(y_pairs, axis=-1)
        outs = []
        for g in range(SSD_GROUPS):
            yg = y[:, g * gw:(g + 1) * gw]
            ms = jnp.mean(yg * yg, axis=-1, keepdims=True)
            outs.append(yg * lax.rsqrt(ms + RMS_EPS) * nw_ref[:, g * gw:(g + 1) * gw])
        o_ref[b, rows, :] = jnp.concatenate(outs, axis=-1).astype(o_ref.dtype)


def _ssd_mixer(z, xbc, dt_raw, conv_w, conv_b, dt_bias, a_log, d_skip, norm_w):
    batch, seq, _ = z.shape
    dt_bias_lanes = jnp.pad(dt_bias.astype(F32), (0, LANES - SSD_HEADS)).reshape(1, LANES)
    a_log_rows = jnp.broadcast_to(jnp.pad(a_log.astype(F32), (0, SSD_HEAD_ROWS - SSD_HEADS))[:, None],
                                  (SSD_HEAD_ROWS, SSD_CHUNK))
    expand = (jnp.arange(SSD_HEAD_ROWS)[:, None] == (jnp.arange(SSD_D_INNER) // SSD_HEAD_DIM)[None, :]).astype(BF16)
    expand = jnp.concatenate([expand] * 3, axis=0)
    return pl.pallas_call(
        _ssd_kernel,
        grid=(seq // (CHUNKS_PER_STEP * SSD_CHUNK),),
        in_specs=[_seq_spec(batch, SSD_CHUNK, SSD_D_INNER), _seq_spec(batch, SSD_CHUNK, SSD_CONV_DIM),
                  _seq_spec(batch, SSD_CHUNK, LANES),
                  _const_spec((SSD_CONV, SSD_CONV_DIM)), _const_spec((1, SSD_CONV_DIM)),
                  _const_spec((1, LANES)), _const_spec((SSD_HEAD_ROWS, SSD_CHUNK)),
                  _const_spec((1, SSD_D_INNER)), _const_spec((1, SSD_D_INNER)),
                  _const_spec((3 * SSD_HEAD_ROWS, SSD_D_INNER))],
        out_specs=_seq_spec(batch, SSD_CHUNK, SSD_D_INNER),
        out_shape=jax.ShapeDtypeStruct((batch, seq, SSD_D_INNER), BF16),
        scratch_shapes=[pltpu.VMEM((batch, SSD_HEADS // 2, LANES, LANES), F32),
                        pltpu.VMEM((batch, SSD_CONV_DIM // LANES, SSD_CHUNK + SUBLANES, LANES), F32)],
        compiler_params=_params(1),
        name="ssd_mixer",
    )(z, xbc, dt_raw, conv_w.astype(F32), conv_b.astype(F32).reshape(1, -1), dt_bias_lanes, a_log_rows,
      jnp.repeat(d_skip.astype(F32), SSD_HEAD_DIM).reshape(1, -1), norm_w.astype(F32).reshape(1, -1), expand)


def _hgrn2_kernel(hq_ref, hf_ref, hi_ref, hg_ref, lower_ref, nw_ref, o_ref, state_ref, attn_ref, *, layer):
    t = HG_CHUNK
    batch = hq_ref.shape[0]
    rows = batch * t

    @pl.when(pl.program_id(0) == 0)
    def _():
        state_ref[...] = jnp.zeros_like(state_ref)

    low = lower_ref[...]
    ex = jnp.exp(low - jnp.max(low, axis=0, keepdims=True))
    sm = ex / jnp.sum(ex, axis=0, keepdims=True)
    lb = jnp.clip(jnp.sum(sm[:layer + 1, :], axis=0, keepdims=True) - sm[0:1, :], 0.0, 1.0)

    row = lax.broadcasted_iota(jnp.int32, (rows, rows), 0)
    col = lax.broadcasted_iota(jnp.int32, (rows, rows), 1)
    causal = (col <= row) & ((row & -t) == (col & -t))
    causal_b = jnp.where(causal, 1.0, 0.0).astype(BF16)
    head_slices = [slice(h * HG_KEY_DIM, (h + 1) * HG_KEY_DIM) for h in range(HG_HEADS)]

    for c in range(CHUNKS_PER_STEP):
        stack = lambda ref, c=c: jnp.concatenate([ref[b, c * t:(c + 1) * t, :] for b in range(batch)], axis=0)
        q = _silu(stack(hq_ref))
        fx = stack(hf_ref)
        e = jnp.exp(-jnp.abs(fx))
        big = 1.0 / (1.0 + e)
        small = e * big
        pos = fx >= 0.0
        log_f = jnp.log(lb + (1.0 - lb) * jnp.where(pos, big, small))
        k = (1.0 - lb) * jnp.where(pos, small, big)
        bc = _dot(jnp.concatenate([causal_b] * 3, axis=1), jnp.concatenate(_split3(log_f), axis=0))
        last_rows = [bc[(b + 1) * t - 1:(b + 1) * t, :] for b in range(batch)]
        b_last = jnp.concatenate([jnp.broadcast_to(r, (t, HG_WIDTH)) for r in last_rows], axis=0)

        mid = 0.5 * b_last
        q_mid = (q * jnp.exp(bc - mid)).astype(BF16)
        k_mid = (k * jnp.exp(mid - bc)).astype(BF16)
        for h, sl in enumerate(head_slices):
            attn_ref[h] = _dot_nt(q_mid[:, sl], k_mid[:, sl])

        @pl.when(jnp.logical_not(jnp.max(-jnp.concatenate(last_rows, axis=0)) <= HG_SAFE_DECAY))
        def _():
            col_id = lax.broadcasted_iota(jnp.int32, (rows, rows), 1)
            row_id = lax.broadcasted_iota(jnp.int32, (rows, HG_KEY_DIM), 0)
            for h, sl in enumerate(head_slices):

                def column(s, acc, q_h=q[:, sl], k_h=k[:, sl], bc_h=bc[:, sl]):
                    k_s = jnp.sum(jnp.where(row_id == s, k_h, 0.0), axis=0, keepdims=True)
                    bc_s = jnp.sum(jnp.where(row_id == s, bc_h, 0.0), axis=0, keepdims=True)
                    w = q_h * k_s * jnp.exp(jnp.minimum(bc_h - bc_s, 0.0))
                    return jnp.where(col_id == s, jnp.sum(w, axis=-1, keepdims=True), acc)

                attn_ref[h] = lax.fori_loop(0, rows, column, jnp.zeros((rows, rows), F32))

        vb = stack(hi_ref)
        q_state = (q * jnp.exp(bc)).astype(BF16)
        k_state = (k * jnp.exp(b_last - bc)).astype(BF16)
        outs = []
        for h, sl in enumerate(head_slices):
            o = _dot(jnp.where(causal, attn_ref[h], 0.0).astype(BF16), vb[:, sl])
            o_inter = []
            for b in range(batch):
                rs = slice(b * t, (b + 1) * t)
                state = state_ref[b, h]
                o_inter.append(_dot_nt(q_state[rs, sl], state.astype(BF16)))
                state_ref[b, h] = (state * jnp.exp(last_rows[b][:, sl])
                                   + _dot_tn(vb[rs, sl], k_state[rs, sl]))
            o = o + jnp.concatenate(o_inter, axis=0)
            ms = jnp.mean(o * o, axis=-1, keepdims=True)
            outs.append(o * lax.rsqrt(ms + RMS_EPS) * nw_ref[...])
        y = jnp.concatenate(outs, axis=-1) * _silu(stack(hg_ref))
        for b in range(batch):
            o_ref[b, c * t:(c + 1) * t, :] = y[b * t:(b + 1) * t, :].astype(o_ref.dtype)


def _hgrn2_mixer(hq, hf, hi, hg, hg_lower, norm_w, layer):
    batch, seq, _ = hq.shape
    spec = _seq_spec(batch, HG_CHUNK, HG_WIDTH)
    return pl.pallas_call(
        functools.partial(_hgrn2_kernel, layer=layer),
        grid=(seq // (CHUNKS_PER_STEP * HG_CHUNK),),
        in_specs=[spec, spec, spec, spec, _const_spec(hg_lower.shape), _const_spec((1, HG_VAL_DIM))],
        out_specs=spec,
        out_shape=jax.ShapeDtypeStruct((batch, seq, HG_WIDTH), BF16),
        scratch_shapes=[pltpu.VMEM((batch, HG_HEADS, HG_VAL_DIM, HG_KEY_DIM), F32),
                        pltpu.VMEM((HG_HEADS, batch * HG_CHUNK, batch * HG_CHUNK), F32)],
        compiler_params=_params(1),
        name="hgrn2_mixer",
    )(hq, hf, hi, hg, hg_lower.astype(F32), norm_w.astype(F32).reshape(1, -1))


def _swa_kernel(sinks_ref, q_ref, kp_ref, kc_ref, vp_ref, vc_ref, o_ref):
    t = SWA_BLOCK
    batch = q_ref.shape[0]
    group = SWA_Q_HEADS // SWA_KV_HEADS
    rows = group * t

    lane_kv = lax.broadcasted_iota(jnp.int32, (2 * t, LANES), 1)
    lo_kv = lane_kv < SWA_HEAD_DIM
    lane_q = lax.broadcasted_iota(jnp.int32, (t, LANES), 1)
    lo_q = lane_q < SWA_HEAD_DIM
    row = lax.broadcasted_iota(jnp.int32, (rows, 2 * t), 0) & (t - 1)
    col = lax.broadcasted_iota(jnp.int32, (rows, 2 * t), 1)
    band = (col > row) & (col <= row + t)
    head_of_row = lax.broadcasted_iota(jnp.int32, (rows, 1), 0) // t

    for c, qrows, b in _chunks_of_step(t, batch):
        q = q_ref[b, qrows, :].astype(F32) * (SWA_HEAD_DIM ** -0.5)
        if c == 0:
            visible = band & ((col >= t) | (pl.program_id(0) > 0))
            kcat = jnp.concatenate([kp_ref[b], kc_ref[b, 0:t, :]], axis=0).astype(F32)
            vcat = jnp.concatenate([vp_ref[b], vc_ref[b, 0:t, :]], axis=0).astype(F32)
        else:
            visible = band
            kcat = kc_ref[b, (c - 1) * t:(c + 1) * t, :].astype(F32)
            vcat = vc_ref[b, (c - 1) * t:(c + 1) * t, :].astype(F32)
        k_swap = pltpu.roll(kcat, SWA_HEAD_DIM, axis=1)
        v_swap = pltpu.roll(vcat, SWA_HEAD_DIM, axis=1)
        outs = []
        for g in range(SWA_KV_HEADS):
            k_dup = (jnp.where(lo_kv, kcat, k_swap) if g == 0 else jnp.where(lo_kv, k_swap, kcat)).astype(BF16)
            v_dup = (jnp.where(lo_kv, vcat, v_swap) if g == 0 else jnp.where(lo_kv, v_swap, vcat)).astype(BF16)
            pieces = []
            sink = jnp.zeros((rows, 1), F32)
            for hh in range(group):
                head = g * group + hh
                q_pair = q[:, (head // 2) * LANES:(head // 2 + 1) * LANES]
                pieces.append(jnp.where(lo_q if head % 2 == 0 else jnp.logical_not(lo_q), q_pair, 0.0))
                sink = jnp.where(head_of_row == hh, sinks_ref[head], sink)
            s = _dot_nt(jnp.concatenate(pieces, axis=0).astype(BF16), k_dup)
            s = jnp.where(visible, s, MASK_VALUE)
            m = jnp.maximum(jnp.max(s, axis=-1, keepdims=True), sink)
            p = jnp.exp(s - m)
            denom = jnp.sum(p, axis=-1, keepdims=True) + jnp.exp(sink - m)
            o = _dot((p / denom).astype(BF16), v_dup)
            for pp in range(group // 2):
                outs.append(jnp.where(lo_q, o[2 * pp * t:(2 * pp + 1) * t, :], o[(2 * pp + 1) * t:(2 * pp + 2) * t, :]))
        o_ref[b, qrows, :] = jnp.concatenate(outs, axis=-1).astype(o_ref.dtype)


def _swa_attention(q, k, v, sinks):
    batch, seq, _ = q.shape
    cur = lambda n: pl.BlockSpec((batch, CHUNKS_PER_STEP * SWA_BLOCK, n), lambda i: (0, i, 0))
    prev = lambda n: pl.BlockSpec((batch, SWA_BLOCK, n), lambda i: (0, jnp.maximum(CHUNKS_PER_STEP * i - 1, 0), 0))
    return pl.pallas_call(
        _swa_kernel,
        grid=(seq // (CHUNKS_PER_STEP * SWA_BLOCK),),
        in_specs=[pl.BlockSpec(memory_space=pltpu.SMEM), cur(SWA_Q_DIM),
                  prev(SWA_KV_DIM), cur(SWA_KV_DIM), prev(SWA_KV_DIM), cur(SWA_KV_DIM)],
        out_specs=cur(SWA_Q_DIM),
        out_shape=jax.ShapeDtypeStruct((batch, seq, SWA_Q_DIM), BF16),
        compiler_params=_params(1),
        name="swa_attention",
    )(sinks.astype(F32), q, k, k, v, v)


def _rglru_kernel(gate_ref, xr_ref, cw_ref, cb_ref, wg_ref, bg_ref, lam_ref, o_ref, h_ref, xcat_ref):
    t = RG_CHUNK
    batch = gate_ref.shape[0]
    groups = t // SUBLANES

    @pl.when(pl.program_id(0) == 0)
    def _():
        h_ref[...] = jnp.zeros_like(h_ref)
        xcat_ref[:, :, 0:SUBLANES, :] = jnp.zeros((batch, RG_WIDTH // LANES, SUBLANES, LANES), F32)

    cw = cw_ref[...]
    cb = cb_ref[...]
    decay_rate = (-RG_C) * _softplus(-lam_ref[...])
    sub = lax.broadcasted_iota(jnp.int32, (groups, SUBLANES, LANES), 1)
    for _, rows, b in _chunks_of_step(t, batch):
        xc = _causal_conv(xr_ref[b, rows, :], xcat_ref, b, cw, cb, RG_CONV, t)
        gates = _dot(xc.astype(BF16), wg_ref[...]) + bg_ref[...]
        log_a = decay_rate * _sigmoid(gates[:, :RG_WIDTH])
        a = jnp.exp(log_a)
        th = jnp.tanh(log_a)
        u = jnp.sqrt(jnp.maximum(-2.0 * th / (1.0 - th), 0.0)) * (_sigmoid(gates[:, RG_WIDTH:]) * xc)
        outs = []
        for c in range(RG_WIDTH // LANES):
            sl = slice(c * LANES, (c + 1) * LANES)
            a_c = a[:, sl].reshape(groups, SUBLANES, LANES)
            u_c = u[:, sl].reshape(groups, SUBLANES, LANES)
            d = 1
            while d < SUBLANES:
                keep = sub >= d
                u_c = u_c + a_c * jnp.where(keep, pltpu.roll(u_c, d, axis=1), 0.0)
                a_c = a_c * jnp.where(keep, pltpu.roll(a_c, d, axis=1), 1.0)
                d *= 2
            h_prev = h_ref[b, 0:1, sl]
            h_groups = []
            for g in range(groups):
                h_g = u_c[g] + a_c[g] * h_prev
                h_prev = h_g[SUBLANES - 1:SUBLANES, :]
                h_groups.append(h_g)
            h_ref[b, 0:1, sl] = h_prev
            outs.append(jnp.concatenate(h_groups, axis=0))
        o_ref[b, rows, :] = (jnp.concatenate(outs, axis=-1) * _gelu_tanh(gate_ref[b, rows, :])).astype(o_ref.dtype)


def _block_diag(w):
    nb, d, _ = w.shape
    eye = jnp.eye(nb, dtype=w.dtype)
    return (eye[:, None, :, None] * w[:, :, None, :]).reshape(nb * d, nb * d)


def _rglru_mixer(gate, xr, conv_w, conv_b, wa, ba, wx, bx, lam):
    batch, seq, _ = gate.shape
    spec = _seq_spec(batch, RG_CHUNK, RG_WIDTH)
    w_gates = jnp.concatenate([_block_diag(wa.astype(F32)), _block_diag(wx.astype(F32))], axis=1).astype(BF16)
    b_gates = jnp.concatenate([ba, bx]).astype(F32).reshape(1, -1)
    return pl.pallas_call(
        _rglru_kernel,
        grid=(seq // (CHUNKS_PER_STEP * RG_CHUNK),),
        in_specs=[spec, spec, _const_spec((RG_CONV, RG_WIDTH)), _const_spec((1, RG_WIDTH)),
                  _const_spec((RG_WIDTH, 2 * RG_WIDTH)), _const_spec((1, 2 * RG_WIDTH)),
                  _const_spec((1, RG_WIDTH))],
        out_specs=spec,
        out_shape=jax.ShapeDtypeStruct((batch, seq, RG_WIDTH), BF16),
        scratch_shapes=[pltpu.VMEM((batch, SUBLANES, RG_WIDTH), F32),
                        pltpu.VMEM((batch, RG_WIDTH // LANES, RG_CHUNK + SUBLANES, LANES), F32)],
        compiler_params=_params(1),
        name="rglru_mixer",
    )(gate, xr, conv_w.astype(F32), conv_b.astype(F32).reshape(1, -1), w_gates, b_gates,
      lam.astype(F32).reshape(1, -1))


def _ffn_up(xb, wup_ref, cw_ref, cb_ref, carry_ref, h_ref, act_ref):
    tm = ROW_TILE
    halo = SUBLANES
    fc = FFN_COL_CHUNK
    for c in range(FFN_DIM // fc):
        h = [_dot(xb, wup_ref[:, part * FFN_DIM + c * fc:part * FFN_DIM + (c + 1) * fc]) for part in range(2)]
        for s in range(fc // LANES):
            conv = []
            for part in range(2):
                col = part * FFN_DIM + c * fc + s * LANES
                hb = h_ref.at[c % 2, part, s]
                hb[0:halo, :] = carry_ref[:, col:col + LANES]
                hb[halo:halo + tm, :] = h[part][:, s * LANES:(s + 1) * LANES]
                carry_ref[:, col:col + LANES] = hb[tm:tm + halo, :]
                y = cb_ref[:, col:col + LANES] + cw_ref[2:3, col:col + LANES] * hb[halo:halo + tm, :]
                for k in range(1, FFN_CONV):
                    y = y + cw_ref[2 - k:3 - k, col:col + LANES] * hb[halo - k:halo - k + tm, :]
                conv.append(y)
            act_col = c * fc + s * LANES
            act_ref[:, act_col:act_col + LANES] = (_silu(conv[0]) * conv[1]).astype(BF16)


def _tail_kernel(ya0_ref, yb0_ref, x0_ref, yan_ref, ybn_ref, xn_ref, wout_ref, lng_ref, lnb_ref, wup_ref, cw_ref,
                 cb_ref, wdown_ref, o_ref, carry_ref, h_ref, act_ref, x1_ref, *, tiles_per_seq):
    j = pl.program_id(0)

    def head(ya_ref, yb_ref, x_ref):
        half = ya_ref.shape[1]
        m = _dot(ya_ref[...], wout_ref[0:half, :]) + _dot(yb_ref[...], wout_ref[half:2 * half, :])
        return _layer_norm(ALPHA * x_ref[...] + m, lng_ref[0:1, :], lnb_ref[0:1, :])

    @pl.when(j == 0)
    def _():
        x1_ref[0] = head(ya0_ref, yb0_ref, x0_ref)

    @pl.when(lax.rem(j, tiles_per_seq) == 0)
    def _():
        carry_ref[...] = jnp.zeros_like(carry_ref)

    def step(cur, nxt):
        _ffn_up(x1_ref[cur].astype(BF16), wup_ref, cw_ref, cb_ref, carry_ref, h_ref, act_ref)
        x1_ref[nxt] = head(yan_ref, ybn_ref, xn_ref)
        f = _dot(act_ref[...], wdown_ref[...])
        o_ref[...] = _layer_norm(ALPHA * x1_ref[cur] + f, lng_ref[1:2, :], lnb_ref[1:2, :])

    @pl.when(lax.rem(j, 2) == 0)
    def _():
        step(0, 1)

    @pl.when(lax.rem(j, 2) == 1)
    def _():
        step(1, 0)


def _layer_tail(ya, yb, x, w_out_all, j, ln_g_all, ln_b_all, w_up_all, conv_w_all, conv_b_all, w_down_all, layer, seq):
    rows, d = x.shape
    half = ya.shape[1]
    ntile = rows // ROW_TILE
    first = lambda n: pl.BlockSpec((ROW_TILE, n), lambda t: (0, 0), pipeline_mode=pl.Buffered(1))
    nxt = lambda n: pl.BlockSpec((ROW_TILE, n), lambda t: (jnp.minimum(t + 1, ntile - 1), 0))
    return pl.pallas_call(
        functools.partial(_tail_kernel, tiles_per_seq=seq // ROW_TILE),
        grid=(ntile,),
        in_specs=[first(half), first(half), first(d), nxt(half), nxt(half), nxt(d),
                  _layer_spec((2 * half, d), j), _layer_spec((2, d), layer), _layer_spec((2, d), layer),
                  _layer_spec((d, 2 * FFN_DIM), layer), _layer_spec((FFN_CONV, 2 * FFN_DIM), layer),
                  _layer_spec((1, 2 * FFN_DIM), layer), _layer_spec((FFN_DIM, d), layer)],
        out_specs=pl.BlockSpec((ROW_TILE, d), lambda t: (t, 0)),
        out_shape=jax.ShapeDtypeStruct((rows, d), F32),
        scratch_shapes=[pltpu.VMEM((SUBLANES, 2 * FFN_DIM), F32),
                        pltpu.VMEM((2, 2, FFN_COL_CHUNK // LANES, ROW_TILE + SUBLANES, LANES), F32),
                        pltpu.VMEM((ROW_TILE, FFN_DIM), BF16),
                        pltpu.VMEM((2, ROW_TILE, d), F32)],
        compiler_params=_params(1),
        name="layer_tail",
    )(ya, yb, x, ya, yb, x, w_out_all, ln_g_all, ln_b_all, w_up_all, conv_w_all, conv_b_all, w_down_all)


AB_SPLITS = (SSD_D_INNER, SSD_CONV_DIM, LANES, HG_WIDTH, HG_WIDTH, HG_WIDTH, HG_WIDTH)
CD_SPLITS = (SWA_Q_DIM, SWA_KV_DIM, SWA_KV_DIM, RG_WIDTH, RG_WIDTH)
AB_DTYPES = (F32, F32, F32, F32, F32, BF16, F32)
CD_DTYPES = (BF16, BF16, BF16, F32, F32)


def _ab_weight(w_in):
    a = SSD_D_INNER + SSD_CONV_DIM
    dt_cols = jnp.pad(w_in[..., a:a + SSD_HEADS], ((0, 0), (0, 0), (0, LANES - SSD_HEADS)))
    return jnp.concatenate([w_in[..., :a], dt_cols, w_in[..., a + SSD_HEADS:]], axis=-1).astype(BF16)


def kernel(x, ab_w_in, ssd_conv_w, ssd_conv_b, ssd_dt_bias, ssd_a_log, ssd_d, ssd_norm_w, hg_lower, hg_norm_w, ab_w_out, cd_w_in, swa_sinks, rg_conv_w, rg_conv_b, rg_wa, rg_ba, rg_wx, rg_bx, rg_lambda, cd_w_out, ffn_w_up, ffn_conv_w, ffn_conv_b, ffn_w_down, ln_g, ln_b):
    batch, seq, d = x.shape
    assert d == D_MODEL and seq % ROW_TILE == 0
    rows = batch * seq
    seq_view = lambda arrs: [a.reshape(batch, seq, a.shape[-1]) for a in arrs]
    xr = x.reshape(rows, d).astype(F32)
    ab_w_in_b, cd_w_in_b = _ab_weight(ab_w_in), cd_w_in.astype(BF16)
    ab_w_out_b, cd_w_out_b = ab_w_out.astype(BF16), cd_w_out.astype(BF16)
    w_up_b, w_down_b = ffn_w_up.astype(BF16), ffn_w_down.astype(BF16)
    conv_w, conv_b = ffn_conv_w.astype(F32), ffn_conv_b.astype(F32).reshape(DEPTH, 1, 2 * FFN_DIM)
    ln_g, ln_b = ln_g.astype(F32), ln_b.astype(F32)
    for layer in range(DEPTH):
        j = layer // 2
        if layer % 2 == 0:
            z, xbc, dt_raw, hq, hf, hi, hg = seq_view(_inproj(xr, ab_w_in_b, j, AB_SPLITS, AB_DTYPES))
            ya = _ssd_mixer(z, xbc, dt_raw, ssd_conv_w[j], ssd_conv_b[j], ssd_dt_bias[j], ssd_a_log[j], ssd_d[j],
                            ssd_norm_w[j])
            yb = _hgrn2_mixer(hq, hf, hi, hg, hg_lower, hg_norm_w[j], j)
            w_out_b = ab_w_out_b
        else:
            q, k, v, gate, xg = seq_view(_inproj(xr, cd_w_in_b, j, CD_SPLITS, CD_DTYPES))
            ya = _swa_attention(q, k, v, swa_sinks[j])
            yb = _rglru_mixer(gate, xg, rg_conv_w[j], rg_conv_b[j], rg_wa[j], rg_ba[j], rg_wx[j], rg_bx[j],
                              rg_lambda[j])
            w_out_b = cd_w_out_b
        xr = _layer_tail(ya.reshape(rows, -1), yb.reshape(rows, -1), xr, w_out_b, j, ln_g, ln_b,
                         w_up_b, conv_w, conv_b, w_down_b, layer, seq)
    return xr.reshape(batch, seq, d).astype(x.dtype)
```
